```python
import jax, jax.numpy as jnp
from jax import lax
import numpy as np

D_MODEL = 1024
BATCH = 4
SEQ = 8192
DEPTH = 1

MLA_HEADS = 8
MLA_NOPE = 128
MLA_ROPE = 64
MLA_V = 128
MLA_Q_LORA = 256
MLA_KV_LORA = 256
ROPE_THETA = 10000.0
FOX_HEADS = 16
FOX_HD = 64
ATTN_BLOCK = 128
IN_SPLITS = (MLA_Q_LORA, MLA_KV_LORA, MLA_ROPE,
             FOX_HEADS * FOX_HD, FOX_HEADS * FOX_HD, FOX_HEADS * FOX_HD, FOX_HEADS,
             D_MODEL, D_MODEL)
IN_COLS = sum(IN_SPLITS)
IN_OFFSETS = tuple(int(v) for v in np.cumsum(IN_SPLITS)[:-1])
N_EXPERTS = 32
TOP_K = 4
EXPERT_FF = D_MODEL
SWIGLU_LIMIT = 7.0
SWIGLU_ALPHA = 1.702
MOE_BLOCK = 128
PLE_DIM = 256
EPS = 1e-6

kernel_name = "hybrid_mla_fox_moe_block"


def rms_norm(x, g):
    xf = x.astype(jnp.float32)
    y = xf * lax.rsqrt(jnp.mean(xf * xf, axis=-1, keepdims=True) + EPS)
    return (y * g.astype(jnp.float32)).astype(x.dtype)


def rope_tables(seq_len):
    pos = jnp.arange(seq_len, dtype=jnp.float32)
    inv_freq = ROPE_THETA ** (-jnp.arange(0, MLA_ROPE, 2, dtype=jnp.float32) / MLA_ROPE)
    ang = pos[:, None] * inv_freq[None, :]
    return jnp.cos(ang), jnp.sin(ang)


def apply_rope(x, cos, sin):
    half = x.shape[-1] // 2
    xf = x.astype(jnp.float32)
    x1, x2 = xf[..., :half], xf[..., half:]
    c, s = cos[:, None, :], sin[:, None, :]
    return jnp.concatenate([x1 * c - x2 * s, x2 * c + x1 * s], axis=-1).astype(x.dtype)


def causal_block_attention(q, k, v, scale, log_f_cum=None):
    B, H, S, _ = q.shape
    dv = v.shape[-1]
    n_blocks = S // ATTN_BLOCK
    offs = jnp.arange(ATTN_BLOCK, dtype=jnp.int32)

    def one_query_block(i):
        q0 = i * ATTN_BLOCK
        q_i = lax.dynamic_slice_in_dim(q, q0, ATTN_BLOCK, axis=2)
        q_pos = q0 + offs
        if log_f_cum is not None:
            fq = lax.dynamic_slice_in_dim(log_f_cum, q0, ATTN_BLOCK, axis=2)

        def body(j, carry):
            m, l, acc = carry
            k0 = j * ATTN_BLOCK
            k_j = lax.dynamic_slice_in_dim(k, k0, ATTN_BLOCK, axis=2)
            v_j = lax.dynamic_slice_in_dim(v, k0, ATTN_BLOCK, axis=2)
            s = jnp.einsum("bhqd,bhkd->bhqk", q_i, k_j, preferred_element_type=jnp.float32) * scale
            if log_f_cum is not None:
                fk = lax.dynamic_slice_in_dim(log_f_cum, k0, ATTN_BLOCK, axis=2)
                s = s + fq[..., :, None] - fk[..., None, :]
            s = jnp.where(q_pos[:, None] >= (k0 + offs)[None, :], s, -jnp.inf)
            m_new = jnp.maximum(m, jnp.max(s, axis=-1))
            corr = jnp.exp(m - m_new)
            pr = jnp.exp(s - m_new[..., None])
            l_new = l * corr + jnp.sum(pr, axis=-1)
            acc_new = acc * corr[..., None] + jnp.einsum(
                "bhqk,bhkd->bhqd", pr.astype(v.dtype), v_j, preferred_element_type=jnp.float32)
            return m_new, l_new, acc_new

        init = (jnp.full((B, H, ATTN_BLOCK), -jnp.inf, jnp.float32),
                jnp.zeros((B, H, ATTN_BLOCK), jnp.float32),
                jnp.zeros((B, H, ATTN_BLOCK, dv), jnp.float32))
        _, l, acc = lax.fori_loop(0, i + 1, body, init)
        return (acc / l[..., None]).astype(q.dtype)

    out = lax.map(one_query_block, jnp.arange(n_blocks, dtype=jnp.int32))
    return out.transpose(1, 2, 0, 3, 4).reshape(B, H, S, dv)


def hybrid_mixer(h, cos, sin, w_in, q_a_norm, w_uq, kv_a_norm, w_ukv,
                 mla_q_nope_norm, mla_q_rope_norm, mla_k_nope_norm, mla_k_rope_norm,
                 fox_q_norm, fox_k_norm, fox_f_bias, w_o_mla, w_o_fox, w_o):
    B, S, _ = h.shape
    z = h @ w_in
    c_q, c_kv, k_r, fq, fk, fv, f_logit, g_mla, g_fox = jnp.split(z, IN_OFFSETS, axis=-1)

    q = (rms_norm(c_q, q_a_norm) @ w_uq).reshape(B, S, MLA_HEADS, MLA_NOPE + MLA_ROPE)
    kv = (rms_norm(c_kv, kv_a_norm) @ w_ukv).reshape(B, S, MLA_HEADS, MLA_NOPE + MLA_V)
    q_nope = rms_norm(q[..., :MLA_NOPE], mla_q_nope_norm)
    q_rope = apply_rope(rms_norm(q[..., MLA_NOPE:], mla_q_rope_norm), cos, sin)
    k_nope = rms_norm(kv[..., :MLA_NOPE], mla_k_nope_norm)
    v_mla = kv[..., MLA_NOPE:]
    k_rope = apply_rope(rms_norm(k_r, mla_k_rope_norm)[:, :, None, :], cos, sin)
    q_mla = jnp.concatenate([q_nope, q_rope], axis=-1)
    k_mla = jnp.concatenate(
        [k_nope, jnp.broadcast_to(k_rope, (B, S, MLA_HEADS, MLA_ROPE))], axis=-1)
    o_mla = causal_block_attention(q_mla.transpose(0, 2, 1, 3), k_mla.transpose(0, 2, 1, 3),
                                   v_mla.transpose(0, 2, 1, 3), (MLA_NOPE + MLA_ROPE) ** -0.5)
    o_mla = o_mla.transpose(0, 2, 1, 3).reshape(B, S, MLA_HEADS * MLA_V)

    fq = rms_norm(fq.reshape(B, S, FOX_HEADS, FOX_HD), fox_q_norm)
    fk = rms_norm(fk.reshape(B, S, FOX_HEADS, FOX_HD), fox_k_norm)
    fv = fv.reshape(B, S, FOX_HEADS, FOX_HD)
    log_f = jax.nn.log_sigmoid(f_logit.astype(jnp.float32) + fox_f_bias.astype(jnp.float32))
    f_cum = lax.cumsum(log_f, axis=1).transpose(0, 2, 1)
    o_fox = causal_block_attention(fq.transpose(0, 2, 1, 3), fk.transpose(0, 2, 1, 3),
                                   fv.transpose(0, 2, 1, 3), FOX_HD ** -0.5, log_f_cum=f_cum)
    o_fox = o_fox.transpose(0, 2, 1, 3).reshape(B, S, FOX_HEADS * FOX_HD)

    mixed = jax.nn.sigmoid(g_mla) * (o_mla @ w_o_mla) + jax.nn.sigmoid(g_fox) * (o_fox @ w_o_fox)
    return mixed @ w_o


def moe_swiglu(h, w_router, b_router, w_up, b_up, w_down, b_down):
    B, S, D = h.shape
    N = B * S
    NK = N * TOP_K
    xf = h.reshape(N, D)
    logits = jnp.einsum("nd,de->ne", xf, w_router, preferred_element_type=jnp.float32) \
        + b_router.astype(jnp.float32)
    top_val, top_idx = lax.top_k(logits, TOP_K)
    gates = jax.nn.softmax(top_val, axis=-1)

    expert_of = top_idx.reshape(-1).astype(jnp.int32)
    token_of = jnp.repeat(jnp.arange(N, dtype=jnp.int32), TOP_K)
    gate_of = gates.reshape(-1)
    order = jnp.argsort(expert_of)
    e_sorted = expert_of[order]
    counts = jnp.bincount(expert_of, length=N_EXPERTS).astype(jnp.int32)
    starts = jnp.cumsum(counts) - counts
    padded = (counts + MOE_BLOCK - 1) // MOE_BLOCK * MOE_BLOCK
    pad_ends = jnp.cumsum(padded)
    pad_starts = pad_ends - padded
    dest = pad_starts[e_sorted] + (jnp.arange(NK, dtype=jnp.int32) - starts[e_sorted])
    n_rows = NK + N_EXPERTS * MOE_BLOCK
    n_blocks = n_rows // MOE_BLOCK
    row_token = jnp.full((n_rows,), N, jnp.int32).at[dest].set(token_of[order])
    row_gate = jnp.zeros((n_rows,), jnp.float32).at[dest].set(gate_of[order])
    block_expert = jnp.minimum(
        jnp.searchsorted(pad_ends, jnp.arange(n_blocks, dtype=jnp.int32) * MOE_BLOCK, side="right"),
        N_EXPERTS - 1)
    x_rows = jnp.concatenate([xf, jnp.zeros((1, D), xf.dtype)], axis=0)[row_token]
    x_rows = x_rows.reshape(n_blocks, MOE_BLOCK, D)

    def expert_block(args):
        xb, e = args
        gu = xb @ w_up[e] + b_up[e]
        g, u = gu[..., :EXPERT_FF], gu[..., EXPERT_FF:]
        g = jnp.minimum(g, SWIGLU_LIMIT)
        u = jnp.clip(u, -SWIGLU_LIMIT, SWIGLU_LIMIT)
        act = g * jax.nn.sigmoid(SWIGLU_ALPHA * g) * (u + 1.0)
        return act @ w_down[e] + b_down[e]

    y_rows = lax.map(expert_block, (x_rows, block_expert)).reshape(n_rows, D)
    y = jnp.zeros((N + 1, D), jnp.float32).at[row_token].add(
        y_rows.astype(jnp.float32) * row_gate[:, None])[:N]
    return y.reshape(B, S, D).astype(h.dtype)


def setup_inputs(seed: int = 0) -> dict:
    key = jax.random.key(seed)
    ks = jax.random.split(key, 28)
    L = DEPTH
    f32 = jnp.float32

    def w(k, shape, fan_in):
        return jax.random.normal(k, shape, f32) * fan_in ** -0.5

    def gain(k, n):
        return 1.0 + 0.05 * jax.random.normal(k, (L, n), f32)

    def small(k, shape):
        return 0.01 * jax.random.normal(k, shape, f32)

    return {
        "x": jax.random.normal(ks[0], (BATCH, SEQ, D_MODEL), f32),
        "p": jax.random.normal(ks[1], (DEPTH, BATCH, SEQ, PLE_DIM), f32),
        "attn_norm": gain(ks[2], D_MODEL),
        "w_in": w(ks[3], (L, D_MODEL, IN_COLS), D_MODEL),
        "q_a_norm": gain(ks[4], MLA_Q_LORA),
        "w_uq": w(ks[5], (L, MLA_Q_LORA, MLA_HEADS * (MLA_NOPE + MLA_ROPE)), MLA_Q_LORA),
        "kv_a_norm": gain(ks[6], MLA_KV_LORA),
        "w_ukv": w(ks[7], (L, MLA_KV_LORA, MLA_HEADS * (MLA_NOPE + MLA_V)), MLA_KV_LORA),
        "mla_q_nope_norm": gain(ks[8], MLA_NOPE),
        "mla_q_rope_norm": gain(ks[9], MLA_ROPE),
        "mla_k_nope_norm": gain(ks[10], MLA_NOPE),
        "mla_k_rope_norm": gain(ks[11], MLA_ROPE),
        "fox_q_norm": gain(ks[12], FOX_HD),
        "fox_k_norm": gain(ks[13], FOX_HD),
        "fox_f_bias": jax.random.uniform(ks[14], (L, FOX_HEADS), f32, minval=1.0, maxval=6.0),
        "w_o_mla": w(ks[15], (L, MLA_HEADS * MLA_V, D_MODEL), MLA_HEADS * MLA_V),
        "w_o_fox": w(ks[16], (L, FOX_HEADS * FOX_HD, D_MODEL), FOX_HEADS * FOX_HD),
        "w_o": w(ks[17], (L, D_MODEL, D_MODEL), D_MODEL),
        "ffn_norm": gain(ks[18], D_MODEL),
        "w_router": w(ks[19], (L, D_MODEL, N_EXPERTS), D_MODEL),
        "b_router": small(ks[20], (L, N_EXPERTS)),
        "w_up": w(ks[21], (L, N_EXPERTS, D_MODEL, 2 * EXPERT_FF), D_MODEL),
        "b_up": small(ks[22], (L, N_EXPERTS, 2 * EXPERT_FF)),
        "w_down": w(ks[23], (L, N_EXPERTS, EXPERT_FF, D_MODEL), EXPERT_FF),
        "b_down": small(ks[24], (L, N_EXPERTS, D_MODEL)),
        "ple_norm": gain(ks[25], D_MODEL),
        "w_ple_gate": w(ks[26], (L, D_MODEL, D_MODEL), D_MODEL),
        "w_ple_proj": w(ks[27], (L, PLE_DIM, D_MODEL), PLE_DIM),
    }


def reference(x, p, attn_norm, w_in, q_a_norm, w_uq, kv_a_norm, w_ukv,
              mla_q_nope_norm, mla_q_rope_norm, mla_k_nope_norm, mla_k_rope_norm,
              fox_q_norm, fox_k_norm, fox_f_bias, w_o_mla, w_o_fox, w_o,
              ffn_norm, w_router, b_router, w_up, b_up, w_down, b_down,
              ple_norm, w_ple_gate, w_ple_proj):
    cos, sin = rope_tables(x.shape[1])
    for i in range(DEPTH):
        h = rms_norm(x, attn_norm[i])
        x = x + hybrid_mixer(h, cos, sin, w_in[i], q_a_norm[i], w_uq[i], kv_a_norm[i], w_ukv[i],
                             mla_q_nope_norm[i], mla_q_rope_norm[i], mla_k_nope_norm[i],
                             mla_k_rope_norm[i], fox_q_norm[i], fox_k_norm[i], fox_f_bias[i],
                             w_o_mla[i], w_o_fox[i], w_o[i])
        h = rms_norm(x, ffn_norm[i])
        x = x + moe_swiglu(h, w_router[i], b_router[i], w_up[i], b_up[i], w_down[i], b_down[i])
        ple_gate = jax.nn.sigmoid(rms_norm(x, ple_norm[i]) @ w_ple_gate[i])
        x = x + ple_gate * (p[i] @ w_ple_proj[i])
    return x
```

```python
import functools

import numpy as np
import jax
import jax.numpy as jnp
from jax import lax
from jax.experimental import pallas as pl
from jax.experimental.pallas import tpu as pltpu

F32 = jnp.float32
BF16 = jnp.bfloat16
I32 = jnp.int32

EPS = 1e-6
ROPE_THETA = 10000.0
MLA_HEADS = 8
MLA_NOPE = 128
MLA_ROPE = 64
MLA_V = 128
MLA_Q_LORA = 256
MLA_KV_LORA = 256
FOX_HEADS = 16
FOX_HD = 64
N_EXPERTS = 32
TOP_K = 4
SWIGLU_LIMIT = 7.0
SWIGLU_ALPHA = 1.702

LANES = 128
ZA_COLS = 640
FLOGIT_LANE = 64
FOX_DK = 128
MOE_ROWS = 256
NEG_BIG = -1e30
VMEM_LIMIT = 56 * 1024 * 1024

TM_PROJ = 512
TM_DMA = 256
ATTN_TQ = 512
ATTN_TK = 512


def _cparams(sem):
    return pltpu.CompilerParams(dimension_semantics=sem, vmem_limit_bytes=VMEM_LIMIT)


def _full(shape):
    nd = len(shape)
    return pl.BlockSpec(shape, lambda *_: (0,) * nd)


def _rms_rows(x, gain):
    return x * lax.rsqrt(jnp.mean(x * x, axis=-1, keepdims=True) + EPS) * gain


def _split3(x):
    hi = x.astype(BF16)
    r1 = x - hi.astype(F32)
    mid = r1.astype(BF16)
    lo = (r1 - mid.astype(F32)).astype(BF16)
    return hi, mid, lo


def _dot(a, b):
    return jnp.dot(a, b, preferred_element_type=F32)


def _inproj_kernel(x_ref, g_ref, wa_ref, wf_ref, wg_ref, za_ref, zf_ref, zg_ref):
    hb = _rms_rows(x_ref[...], g_ref[...]).astype(BF16)
    za_ref[...] = _dot(hb, wa_ref[...])
    zf_ref[...] = _dot(hb, wf_ref[...]).astype(BF16)
    zg_ref[...] = jax.nn.sigmoid(_dot(hb, wg_ref[...])).astype(BF16)


def _inproj(x2d, gain, wa, wf, wg):
    n, d = x2d.shape
    tm = TM_PROJ
    row = lambda c: pl.BlockSpec((tm, c), lambda i: (i, 0))
    return pl.pallas_call(
        _inproj_kernel,
        grid=(n // tm,),
        in_specs=[row(d), _full(gain.shape), _full(wa.shape), _full(wf.shape), _full(wg.shape)],
        out_specs=[row(wa.shape[1]), row(wf.shape[1]), row(wg.shape[1])],
        out_shape=[jax.ShapeDtypeStruct((n, wa.shape[1]), F32),
                   jax.ShapeDtypeStruct((n, wf.shape[1]), BF16),
                   jax.ShapeDtypeStruct((n, wg.shape[1]), BF16)],
        compiler_params=_cparams(("arbitrary",)),
        name="inproj",
    )(x2d, gain, wa, wf, wg)


def _group_rms(x, gmat, gmat_t, group, gain):
    ss = _dot((x * x).astype(BF16), gmat)
    inv = lax.rsqrt(ss * (1.0 / group) + EPS)
    inv_hi = inv.astype(BF16)
    inv_lo = (inv - inv_hi.astype(F32)).astype(BF16)
    inv_full = _dot(inv_hi, gmat_t) + _dot(inv_lo, gmat_t)
    return x * inv_full * gain


def _rot_half(x):
    c = x.shape[-1]
    lane = lax.broadcasted_iota(I32, x.shape, 1)
    first = (lane & 63) < 32
    return jnp.where(first, pltpu.roll(x, c - 32, 1), pltpu.roll(x, 32, 1))


def _log_sigmoid(x):
    return jnp.minimum(x, 0.0) - jnp.log1p(jnp.exp(-jnp.abs(x)))


def _prep_kernel(za_ref, zf_ref, cos_ref, sin_ref, wuq_ref, wukv_ref,
                 gqa_ref, gkva_ref, gqn_ref, gqr_ref, gkn_ref, gkr_ref, gfq_ref, gfk_ref, fb_ref,
                 g128_ref, g128t_ref, g64_ref, g64t_ref, tri_ref, eq_ref, ek_ref, oneq_ref, onek_ref,
                 qm_ref, km_ref, vm_ref, qf_ref, kf_ref, vf_ref, carry_ref):
    si = pl.program_id(1)
    za = za_ref[...]
    cos = cos_ref[...]
    sin = sin_ref[...]
    g128, g128t = g128_ref[...], g128t_ref[...]
    g64, g64t = g64_ref[...], g64t_ref[...]

    cq = _rms_rows(za[:, 0:MLA_Q_LORA], gqa_ref[...]).astype(BF16)
    ckv = _rms_rows(za[:, MLA_Q_LORA:MLA_Q_LORA + MLA_KV_LORA], gkva_ref[...]).astype(BF16)
    last = za[:, ZA_COLS - LANES:ZA_COLS]
    q = _dot(cq, wuq_ref[...])
    kv = _dot(ckv, wukv_ref[...])
    n_nope = MLA_HEADS * MLA_NOPE
    n_rope = MLA_HEADS * MLA_ROPE
    qn = _group_rms(q[:, :n_nope], g128, g128t, MLA_NOPE, gqn_ref[...])
    qr = _group_rms(q[:, n_nope:], g64[:n_rope], g64t[:, :n_rope], MLA_ROPE, gqr_ref[...])
    cos4 = jnp.concatenate([cos] * (n_rope // LANES), axis=-1)
    sin4 = jnp.concatenate([sin] * (n_rope // LANES), axis=-1)
    qr = qr * cos4 + _rot_half(qr) * sin4
    kn = _group_rms(kv[:, :n_nope], g128, g128t, MLA_NOPE, gkn_ref[...])
    lane = lax.broadcasted_iota(I32, last.shape, 1)
    kr_ss = jnp.sum(jnp.where(lane < MLA_ROPE, last * last, 0.0), axis=-1, keepdims=True)
    kr = last * lax.rsqrt(kr_ss * (1.0 / MLA_ROPE) + EPS) * gkr_ref[...]
    kr = kr * cos + _rot_half(kr) * sin
    kr_b = kr[:, :MLA_ROPE].astype(BF16)
    for h in range(MLA_HEADS):
        qm_ref[0, h, :, 0:MLA_NOPE] = qn[:, h * MLA_NOPE:(h + 1) * MLA_NOPE].astype(BF16)
        qm_ref[0, h, :, MLA_NOPE:MLA_NOPE + MLA_ROPE] = qr[:, h * MLA_ROPE:(h + 1) * MLA_ROPE].astype(BF16)
        km_ref[0, h, :, 0:MLA_NOPE] = kn[:, h * MLA_NOPE:(h + 1) * MLA_NOPE].astype(BF16)
        km_ref[0, h, :, MLA_NOPE:MLA_NOPE + MLA_ROPE] = kr_b
        vm_ref[0, h] = kv[:, n_nope + h * MLA_V:n_nope + (h + 1) * MLA_V].astype(BF16)

    nf = FOX_HEADS * FOX_HD
    zf = zf_ref[...]
    fq = _group_rms(zf[:, 0:nf].astype(F32), g64, g64t, FOX_HD, gfq_ref[...])
    fk = _group_rms(zf[:, nf:2 * nf].astype(F32), g64, g64t, FOX_HD, gfk_ref[...])
    fv = zf[:, 2 * nf:3 * nf]

    @pl.when(si == 0)
    def _():
        carry_ref[...] = jnp.zeros_like(carry_ref)

    logf = _log_sigmoid(last + fb_ref[...])
    l_hi, l_mid, l_lo = _split3(logf)
    tri = tri_ref[...]
    cum = carry_ref[...] + (_dot(tri, l_hi) + _dot(tri, l_mid) + _dot(tri, l_lo))
    tm = cum.shape[0]
    carry_ref[...] = cum[tm - 1:tm, :]
    fcat = jnp.concatenate(_split3(cum), axis=-1)
    augq = _dot(fcat, eq_ref[...]) + oneq_ref[...]
    augk = _dot(fcat, ek_ref[...]) + onek_ref[...]
    for h in range(FOX_HEADS):
        sl = slice(h * FOX_HD, (h + 1) * FOX_HD)
        qf_ref[0, h] = jnp.concatenate([fq[:, sl], augq[:, sl]], axis=-1).astype(BF16)
        kf_ref[0, h] = jnp.concatenate([fk[:, sl], augk[:, sl]], axis=-1).astype(BF16)
        vf_ref[0, h] = fv[:, sl]


def _group_matrix(cols, group):
    m = np.zeros((cols, LANES), np.float32)
    m[np.arange(cols), np.arange(cols) // group] = 1.0
    return m


def _bias_column_matrices():
    nf = FOX_HEADS * FOX_HD
    eq = np.zeros((3 * LANES, nf), np.float32)
    ek = np.zeros((3 * LANES, nf), np.float32)
    oneq = np.zeros((1, nf), np.float32)
    onek = np.zeros((1, nf), np.float32)
    for h in range(FOX_HEADS):
        for t in range(3):
            eq[t * LANES + FLOGIT_LANE + h, h * FOX_HD + t] = 1.0
            ek[t * LANES + FLOGIT_LANE + h, h * FOX_HD + 3 + t] = -1.0
            oneq[0, h * FOX_HD + 3 + t] = 1.0
            onek[0, h * FOX_HD + t] = 1.0
    return eq, ek, oneq, onek


def _prep(za, zf, cos, sin, wuq, wukv, gains, batch, seq):
    tm = TM_PROJ
    nt = seq // tm
    g128 = _group_matrix(MLA_HEADS * MLA_NOPE, MLA_NOPE)
    g64 = _group_matrix(FOX_HEADS * FOX_HD, FOX_HD)
    tri = np.tril(np.ones((tm, tm), np.float32))
    eq, ek, oneq, onek = _bias_column_matrices()
    consts = [jnp.asarray(g128, BF16), jnp.asarray(g128.T, BF16), jnp.asarray(g64, BF16), jnp.asarray(g64.T, BF16),
              jnp.asarray(tri, BF16), jnp.asarray(eq, BF16), jnp.asarray(ek, BF16),
              jnp.asarray(oneq, F32), jnp.asarray(onek, F32)]
    row = lambda c: pl.BlockSpec((tm, c), lambda b, s: (b * nt + s, 0))
    tab = pl.BlockSpec((tm, LANES), lambda b, s: (s, 0))
    head = lambda h, d: pl.BlockSpec((1, h, tm, d), lambda b, s: (b, 0, s, 0))
    hshape = lambda h, d: jax.ShapeDtypeStruct((batch, h, seq, d), BF16)
    dk = MLA_NOPE + MLA_ROPE
    return pl.pallas_call(
        _prep_kernel,
        grid=(batch, nt),
        in_specs=[row(za.shape[1]), row(zf.shape[1]), tab, tab, _full(wuq.shape), _full(wukv.shape)]
                 + [_full(g.shape) for g in gains] + [_full(c.shape) for c in consts],
        out_specs=[head(MLA_HEADS, dk), head(MLA_HEADS, dk), head(MLA_HEADS, MLA_V),
                   head(FOX_HEADS, FOX_DK), head(FOX_HEADS, FOX_DK), head(FOX_HEADS, FOX_HD)],
        out_shape=[hshape(MLA_HEADS, dk), hshape(MLA_HEADS, dk), hshape(MLA_HEADS, MLA_V),
                   hshape(FOX_HEADS, FOX_DK), hshape(FOX_HEADS, FOX_DK), hshape(FOX_HEADS, FOX_HD)],
        scratch_shapes=[pltpu.VMEM((1, LANES), F32)],
        compiler_params=_cparams(("arbitrary", "arbitrary")),
        name="prep",
    )(za, zf, cos, sin, wuq, wukv, *gains, *consts)


def _attn_kernel(q_ref, k_ref, v_ref, o_ref, m_ref, l_ref, acc_ref, *, tq, tk, hpb, dv):
    qi = pl.program_id(2)
    kv_per_q = tq // tk

    def step(hh, j, masked):
        q = q_ref[0, hh]
        k = k_ref[0, hh, pl.ds(pl.multiple_of(j * tk, tk), tk), :]
        v = v_ref[0, hh, pl.ds(pl.multiple_of(j * tk, tk), tk), :]
        s = lax.dot_general(q, k, (((1,), (1,)), ((), ())), preferred_element_type=F32)
        if masked:
            rows = qi * tq + lax.broadcasted_iota(I32, s.shape, 0)
            cols = j * tk + lax.broadcasted_iota(I32, s.shape, 1)
            s = jnp.where(rows >= cols, s, NEG_BIG)
        m_old = m_ref[hh]
        m_new = jnp.maximum(m_old, jnp.max(s, axis=-1, keepdims=True))
        corr = jnp.exp(m_old - m_new)
        p = jnp.exp(s - m_new)
        l_ref[hh] = l_ref[hh] * corr + jnp.sum(p, axis=-1, keepdims=True)
        acc_ref[hh] = acc_ref[hh] * corr + _dot(p.astype(BF16), v)
        m_ref[hh] = m_new

    for hh in range(hpb):
        m_ref[hh] = jnp.full(m_ref.shape[1:], NEG_BIG, F32)
        l_ref[hh] = jnp.zeros(l_ref.shape[1:], F32)
        acc_ref[hh] = jnp.zeros(acc_ref.shape[1:], F32)

        def full_block(j, carry, hh=hh):
            step(hh, j, masked=False)
            return carry

        lax.fori_loop(0, qi * kv_per_q, full_block, 0)
        for jj in range(kv_per_q):
            step(hh, qi * kv_per_q + jj, masked=True)
        o_ref[0, :, hh * dv:(hh + 1) * dv] = (acc_ref[hh] / l_ref[hh]).astype(o_ref.dtype)


def _attention(q, k, v, hpb):
    b, h, s, dk = q.shape
    dv = v.shape[-1]
    tq, tk = ATTN_TQ, ATTN_TK
    assert s % tq == 0 and tq % tk == 0 and h % hpb == 0 and (hpb * dv) % LANES == 0
    kern = functools.partial(_attn_kernel, tq=tq, tk=tk, hpb=hpb, dv=dv)
    return pl.pallas_call(
        kern,
        grid=(b, h // hpb, s // tq),
        in_specs=[pl.BlockSpec((1, hpb, tq, dk), lambda bi, hi, qi: (bi, hi, qi, 0)),
                  pl.BlockSpec((1, hpb, s, dk), lambda bi, hi, qi: (bi, hi, 0, 0)),
                  pl.BlockSpec((1, hpb, s, dv), lambda bi, hi, qi: (bi, hi, 0, 0))],
        out_specs=pl.BlockSpec((1, tq, hpb * dv), lambda bi, hi, qi: (bi, qi, hi)),
        out_shape=jax.ShapeDtypeStruct((b, s, h * dv), BF16),
        scratch_shapes=[pltpu.VMEM((hpb, tq, 1), F32), pltpu.VMEM((hpb, tq, 1), F32),
                        pltpu.VMEM((hpb, tq, dv), F32)],
        compiler_params=_cparams(("arbitrary", "arbitrary", "arbitrary")),
        name="attention",
    )(q, k, v)


def _outproj_kernel(x_ref, om_ref, of_ref, zg_ref, wom_ref, wof_ref, wo_ref, gffn_ref, wr_ref, br_ref,
                    x1_ref, h2_ref, idx_ref, gate_ref):
    d = x_ref.shape[1]
    zg = zg_ref[...].astype(F32)
    mixed = zg[:, :d] * _dot(om_ref[...], wom_ref[...]) + zg[:, d:] * _dot(of_ref[...], wof_ref[...])
    x1 = x_ref[...] + _dot(mixed.astype(BF16), wo_ref[...])
    x1_ref[...] = x1
    h2 = _rms_rows(x1, gffn_ref[...])
    h2_ref[...] = h2
    logits = jnp.dot(h2, wr_ref[...], preferred_element_type=F32, precision=lax.Precision.HIGHEST) + br_ref[...]
    lane = lax.broadcasted_iota(I32, logits.shape, 1)
    lane_f = lane.astype(F32)
    work = jnp.where(lane < N_EXPERTS, logits, -jnp.inf)
    vals, idxs = [], []
    for _ in range(TOP_K):
        m = jnp.max(work, axis=-1, keepdims=True)
        idx = jnp.min(jnp.where(work == m, lane_f, float(LANES)), axis=-1, keepdims=True)
        vals.append(m)
        idxs.append(idx)
        work = jnp.where(lane_f == idx, -jnp.inf, work)
    exps = [jnp.exp(v - vals[0]) for v in vals]
    denom = exps[0] + exps[1] + exps[2] + exps[3]
    idx_out = jnp.zeros(logits.shape, F32)
    gate_out = jnp.zeros(logits.shape, F32)
    for kk in range(TOP_K):
        idx_out = jnp.where(lane == kk, idxs[kk], idx_out)
        gate_out = jnp.where(lane == kk, exps[kk] / denom, gate_out)
    idx_ref[...] = idx_out.astype(I32)
    gate_ref[...] = gate_out


def _outproj(x2d, o_mla, o_fox, zg, wom, wof, wo, gffn, wr, br):
    n, d = x2d.shape
    tm = TM_PROJ
    row = lambda c: pl.BlockSpec((tm, c), lambda i: (i, 0))
    return pl.pallas_call(
        _outproj_kernel,
        grid=(n // tm,),
        in_specs=[row(d), row(o_mla.shape[1]), row(o_fox.shape[1]), row(zg.shape[1]),
                  _full(wom.shape), _full(wof.shape), _full(wo.shape), _full(gffn.shape),
                  _full(wr.shape), _full(br.shape)],
        out_specs=[row(d), row(d), row(LANES), row(LANES)],
        out_shape=[jax.ShapeDtypeStruct((n, d), F32), jax.ShapeDtypeStruct((n, d), F32),
                   jax.ShapeDtypeStruct((n, LANES), I32), jax.ShapeDtypeStruct((n, LANES), F32)],
        compiler_params=_cparams(("arbitrary",)),
        name="outproj",
    )(x2d, o_mla, o_fox, zg, wom, wof, wo, gffn, wr, br)


def _route_kernel(idx_ref, lstrict_ref, ustrict_ref, dest_ref, be_ref, meta_ref, cnt_ref, start_ref, run_ref):
    phase = pl.program_id(0)
    i = pl.program_id(1)
    idx = idx_ref[...]
    lane = lax.broadcasted_iota(I32, idx.shape, 1)
    hit = [lane == idx[:, kk:kk + 1] for kk in range(TOP_K)]
    onehot = jnp.zeros(idx.shape, F32)
    for hk in hit:
        onehot = jnp.where(hk, 1.0, onehot)
    tile_cnt = jnp.sum(onehot, axis=0, keepdims=True)

    @pl.when((phase == 0) & (i == 0))
    def _():
        cnt_ref[...] = jnp.zeros_like(cnt_ref)

    @pl.when(phase == 0)
    def _():
        cnt_ref[...] += tile_cnt

    @pl.when((phase == 1) & (i == 0))
    def _():
        cnt = cnt_ref[...]
        nblk = jnp.floor((cnt + (MOE_ROWS - 1.0)) * (1.0 / MOE_ROWS))
        nblk8 = jnp.broadcast_to(nblk, (8, LANES)).astype(BF16)
        start = _dot(nblk8, ustrict_ref[...])[0:1, :]
        start_ref[...] = start
        run_ref[...] = jnp.zeros_like(run_ref)
        end = start + nblk
        nb = be_ref.shape[0]
        bid = lax.broadcasted_iota(I32, (nb, LANES), 0).astype(F32)
        lane_b = lax.broadcasted_iota(I32, (nb, LANES), 1)
        owned = jnp.where((lane_b < N_EXPERTS) & (end <= bid), 1.0, 0.0)
        be = jnp.minimum(jnp.sum(owned, axis=-1, keepdims=True), N_EXPERTS - 1.0)
        be_ref[...] = jnp.broadcast_to(be, (nb, LANES)).astype(I32)
        pad_lo = start * MOE_ROWS + cnt
        pad_hi = end * MOE_ROWS
        row8 = lax.broadcasted_iota(I32, (8, LANES), 0)
        meta = jnp.where(row8 == 0, pad_lo, jnp.where(row8 == 1, pad_hi, jnp.where(row8 == 2, end, 0.0)))
        meta_ref[...] = meta.astype(I32)

    @pl.when(phase == 1)
    def _():
        rank = _dot(lstrict_ref[...], onehot.astype(BF16))
        slot = start_ref[...] * MOE_ROWS + run_ref[...] + rank
        out = jnp.zeros(idx.shape, F32)
        for kk in range(TOP_K):
            dk = jnp.sum(jnp.where(hit[kk], slot, 0.0), axis=-1, keepdims=True)
            out = jnp.where(lane == kk, dk, out)
        dest_ref[...] = out.astype(I32)
        run_ref[...] += tile_cnt


def _route(idx, n_blocks):
    n = idx.shape[0]
    tm = TM_PROJ
    nt = n // tm
    nb = (n_blocks + 7) // 8 * 8
    lstrict = jnp.asarray(np.tril(np.ones((tm, tm), np.float32), -1), BF16)
    ustrict = jnp.asarray(np.triu(np.ones((LANES, LANES), np.float32), 1), BF16)
    return pl.pallas_call(
        _route_kernel,
        grid=(2, nt),
        in_specs=[pl.BlockSpec((tm, LANES), lambda p, i: (i, 0)), _full(lstrict.shape), _full(ustrict.shape)],
        out_specs=[pl.BlockSpec((tm, LANES), lambda p, i: (i * p, 0)), _full((nb, LANES)), _full((8, LANES))],
        out_shape=[jax.ShapeDtypeStruct((n, LANES), I32), jax.ShapeDtypeStruct((nb, LANES), I32),
                   jax.ShapeDtypeStruct((8, LANES), I32)],
        scratch_shapes=[pltpu.VMEM((1, LANES), F32), pltpu.VMEM((1, LANES), F32), pltpu.VMEM((1, LANES), F32)],
        compiler_params=_cparams(("arbitrary", "arbitrary")),
        name="route",
    )(idx, lstrict, ustrict)


def _dispatch_kernel(meta_ref, dest_ref, h_ref, xs_ref, zero_ref, sem, zsem):
    i = pl.program_id(0)
    tm = h_ref.shape[0]

    def row_copy(t, kk):
        d = dest_ref[t * TOP_K + kk]
        return pltpu.make_async_copy(h_ref.at[pl.ds(t, 1), :], xs_ref.at[pl.ds(d, 1), :], sem)

    def issue(t, carry):
        for kk in range(TOP_K):
            row_copy(t, kk).start()
        return carry

    def drain(t, carry):
        for kk in range(TOP_K):
            row_copy(t, kk).wait()
        return carry

    lax.fori_loop(0, tm, issue, 0)

    @pl.when(i == pl.num_programs(0) - 1)
    def _():
        zero_ref[...] = jnp.zeros_like(zero_ref)

        def pad_copy(r):
            return pltpu.make_async_copy(zero_ref.at[pl.ds(0, 1), :], xs_ref.at[pl.ds(r, 1), :], zsem)

        def blk_copy(b):
            return pltpu.make_async_copy(zero_ref, xs_ref.at[pl.ds(b * MOE_ROWS, MOE_ROWS), :], zsem)

        for e in range(N_EXPERTS):
            lo, hi = meta_ref[0, e], meta_ref[1, e]
            lax.fori_loop(lo, hi, lambda r, c: (pad_copy(r).start(), c)[1], 0)
            lax.fori_loop(lo, hi, lambda r, c: (pad_copy(r).wait(), c)[1], 0)
        n_used = meta_ref[2, N_EXPERTS - 1]
        n_blocks = xs_ref.shape[0] // MOE_ROWS
        lax.fori_loop(n_used, n_blocks, lambda b, c: (blk_copy(b).start(), c)[1], 0)
        lax.fori_loop(n_used, n_blocks, lambda b, c: (blk_copy(b).wait(), c)[1], 0)

    lax.fori_loop(0, tm, drain, 0)


def _dispatch(h2, dest_flat, meta, n_rows):
    n, d = h2.shape
    tm = TM_DMA
    grid_spec = pltpu.PrefetchScalarGridSpec(
        num_scalar_prefetch=1,
        grid=(n // tm,),
        in_specs=[pl.BlockSpec((tm * TOP_K,), lambda i, meta: (i,), memory_space=pltpu.SMEM),
                  pl.BlockSpec((tm, d), lambda i, meta: (i, 0))],
        out_specs=pl.BlockSpec(memory_space=pl.ANY),
        scratch_shapes=[pltpu.VMEM((MOE_ROWS, d), F32), pltpu.SemaphoreType.DMA(()), pltpu.SemaphoreType.DMA(())],
    )
    return pl.pallas_call(
        _dispatch_kernel,
        grid_spec=grid_spec,
        out_shape=jax.ShapeDtypeStruct((n_rows, d), F32),
        compiler_params=_cparams(("arbitrary",)),
        name="dispatch",
    )(meta, dest_flat, h2)


def _experts_kernel(be_ref, xs_ref, wu_ref, bu_ref, wd_ref, bd_ref, y_ref, wu_bf, wd_bf):
    b = pl.program_id(0)
    ff = wd_ref.shape[1]
    prev = be_ref[jnp.maximum(b - 1, 0)]

    @pl.when((b == 0) | (be_ref[b] != prev))
    def _():
        wu_bf[...] = wu_ref[0].astype(BF16)
        wd_bf[...] = wd_ref[0].astype(BF16)

    gu = _dot(xs_ref[...].astype(BF16), wu_bf[...]) + bu_ref[0]
    g = jnp.minimum(gu[:, :ff], SWIGLU_LIMIT)
    u = jnp.clip(gu[:, ff:], -SWIGLU_LIMIT, SWIGLU_LIMIT)
    act = g * jax.nn.sigmoid(SWIGLU_ALPHA * g) * (u + 1.0)
    y_ref[...] = _dot(act.astype(BF16), wd_bf[...]) + bd_ref[0]


def _experts(xs, block_expert, w_up, b_up, w_down, b_down):
    n_rows, d = xs.shape
    e, _, ff2 = w_up.shape
    ff = w_down.shape[1]
    grid_spec = pltpu.PrefetchScalarGridSpec(
        num_scalar_prefetch=1,
        grid=(n_rows // MOE_ROWS,),
        in_specs=[pl.BlockSpec((MOE_ROWS, d), lambda b, be: (b, 0)),
                  pl.BlockSpec((1, d, ff2), lambda b, be: (be[b], 0, 0)),
                  pl.BlockSpec((1, 1, ff2), lambda b, be: (be[b], 0, 0)),
                  pl.BlockSpec((1, ff, d), lambda b, be: (be[b], 0, 0)),
                  pl.BlockSpec((1, 1, d), lambda b, be: (be[b], 0, 0))],
        out_specs=pl.BlockSpec((MOE_ROWS, d), lambda b, be: (b, 0)),
        scratch_shapes=[pltpu.VMEM((d, ff2), BF16), pltpu.VMEM((ff, d), BF16)],
    )
    return pl.pallas_call(
        _experts_kernel,
        grid_spec=grid_spec,
        out_shape=jax.ShapeDtypeStruct((n_rows, d), F32),
        compiler_params=_cparams(("arbitrary",)),
        name="experts",
    )(block_expert, xs, w_up, b_up.reshape(e, 1, ff2), w_down, b_down.reshape(e, 1, d))


def _combine_kernel(dest_ref, x1_ref, gate_ref, p_ref, gple_ref, wpg_ref, wpp_ref, ys_ref, o_ref, rows_ref, sem):
    tm = x1_ref.shape[0]

    def row_copy(t, kk):
        d = dest_ref[t * TOP_K + kk]
        return pltpu.make_async_copy(ys_ref.at[pl.ds(d, 1), :], rows_ref.at[kk, pl.ds(t, 1), :], sem)

    def issue(t, carry):
        for kk in range(TOP_K):
            row_copy(t, kk).start()
        return carry

    def drain(t, carry):
        for kk in range(TOP_K):
            row_copy(t, kk).wait()
        return carry

    lax.fori_loop(0, tm, issue, 0)
    proj = _dot(p_ref[...].astype(BF16), wpp_ref[...])
    lax.fori_loop(0, tm, drain, 0)
    gates = gate_ref[...]
    x2 = x1_ref[...]
    for kk in range(TOP_K):
        x2 = x2 + gates[:, kk:kk + 1] * rows_ref[kk]
    hn = _rms_rows(x2, gple_ref[...]).astype(BF16)
    o_ref[...] = x2 + jax.nn.sigmoid(_dot(hn, wpg_ref[...])) * proj


def _combine(x1, gates, dest_flat, p2d, gple, wpg, wpp, ys):
    n, d = x1.shape
    tm = TM_DMA
    row = lambda c: pl.BlockSpec((tm, c), lambda i: (i, 0))
    return pl.pallas_call(
        _combine_kernel,
        grid=(n // tm,),
        in_specs=[pl.BlockSpec((tm * TOP_K,), lambda i: (i,), memory_space=pltpu.SMEM),
                  row(d), row(LANES), row(p2d.shape[1]), _full(gple.shape), _full(wpg.shape), _full(wpp.shape),
                  pl.BlockSpec(memory_space=pl.ANY)],
        out_specs=row(d),
        out_shape=jax.ShapeDtypeStruct((n, d), F32),
        scratch_shapes=[pltpu.VMEM((TOP_K, tm, d), F32), pltpu.SemaphoreType.DMA(())],
        compiler_params=_cparams(("arbitrary",)),
        name="combine",
    )(dest_flat, x1, gates, p2d, gple, wpg, wpp, ys)


def _rope_tables(seq):
    pos = jnp.arange(seq, dtype=F32)
    inv_freq = ROPE_THETA ** (-jnp.arange(0, MLA_ROPE, 2, dtype=F32) / MLA_ROPE)
    ang = pos[:, None] * inv_freq[None, :]
    c, s = jnp.cos(ang), jnp.sin(ang)
    reps = LANES // MLA_ROPE
    return jnp.tile(jnp.concatenate([c, c], -1), (1, reps)), jnp.tile(jnp.concatenate([-s, s], -1), (1, reps))


def _layer(x2d, p2d, batch, seq, cos, sin, attn_norm, w_in, q_a_norm, w_uq, kv_a_norm, w_ukv,
           mla_q_nope_norm, mla_q_rope_norm, mla_k_nope_norm, mla_k_rope_norm,
           fox_q_norm, fox_k_norm, fox_f_bias, w_o_mla, w_o_fox, w_o,
           ffn_norm, w_router, b_router, w_up, b_up, w_down, b_down, ple_norm, w_ple_gate, w_ple_proj):
    n, d = x2d.shape
    nf = FOX_HEADS * FOX_HD
    row1 = lambda v: v.reshape(1, -1).astype(F32)

    o = np.cumsum([0, MLA_Q_LORA, MLA_KV_LORA, MLA_ROPE, nf, nf, nf, FOX_HEADS, d, d])
    wa = jnp.concatenate([w_in[:, o[0]:o[3]], w_in[:, o[6]:o[7]],
                          jnp.zeros((d, ZA_COLS - int(o[3]) - FOX_HEADS), w_in.dtype)], axis=1).astype(BF16)
    wf = w_in[:, o[3]:o[6]].astype(BF16)
    wg = w_in[:, o[7]:o[9]].astype(BF16)
    za, zf, zg = _inproj(x2d, row1(attn_norm), wa, wf, wg)

    wq3 = w_uq.reshape(MLA_Q_LORA, MLA_HEADS, MLA_NOPE + MLA_ROPE)
    wuq = jnp.concatenate([wq3[:, :, :MLA_NOPE].reshape(MLA_Q_LORA, -1),
                           wq3[:, :, MLA_NOPE:].reshape(MLA_Q_LORA, -1)], axis=1).astype(BF16)
    wkv3 = w_ukv.reshape(MLA_KV_LORA, MLA_HEADS, MLA_NOPE + MLA_V)
    wukv = jnp.concatenate([wkv3[:, :, :MLA_NOPE].reshape(MLA_KV_LORA, -1),
                            wkv3[:, :, MLA_NOPE:].reshape(MLA_KV_LORA, -1)], axis=1).astype(BF16)
    mla_scale = (MLA_NOPE + MLA_ROPE) ** -0.5
    fox_scale = FOX_HD ** -0.5
    gkr = jnp.concatenate([mla_k_rope_norm.astype(F32), jnp.zeros((LANES - MLA_ROPE,), F32)])
    fb = jnp.zeros((LANES,), F32).at[FLOGIT_LANE:FLOGIT_LANE + FOX_HEADS].set(fox_f_bias.astype(F32))
    gains = [row1(q_a_norm), row1(kv_a_norm),
             row1(jnp.tile(mla_q_nope_norm, MLA_HEADS) * mla_scale),
             row1(jnp.tile(mla_q_rope_norm, MLA_HEADS) * mla_scale),
             row1(jnp.tile(mla_k_nope_norm, MLA_HEADS)), row1(gkr),
             row1(jnp.tile(fox_q_norm, FOX_HEADS) * fox_scale), row1(jnp.tile(fox_k_norm, FOX_HEADS)), row1(fb)]
    qm, km, vm, qf, kf, vf = _prep(za, zf, cos, sin, wuq, wukv, gains, batch, seq)

    o_mla = _attention(qm, km, vm, hpb=1).reshape(n, MLA_HEADS * MLA_V)
    o_fox = _attention(qf, kf, vf, hpb=2).reshape(n, nf)

    wr = jnp.concatenate([w_router.astype(F32), jnp.zeros((d, LANES - N_EXPERTS), F32)], axis=1)
    br = jnp.concatenate([b_router.astype(F32), jnp.zeros((LANES - N_EXPERTS,), F32)]).reshape(1, LANES)
    x1, h2, idx, gates = _outproj(x2d, o_mla, o_fox, zg, w_o_mla.astype(BF16), w_o_fox.astype(BF16),
                                  w_o.astype(BF16), row1(ffn_norm), wr, br)

    n_blocks = n * TOP_K // MOE_ROWS + N_EXPERTS
    dest, be, meta = _route(idx, n_blocks)
    dest_flat = dest[:, :TOP_K].reshape(-1)
    xs = _dispatch(h2, dest_flat, meta, n_blocks * MOE_ROWS)
    ys = _experts(xs, be[:n_blocks, 0], w_up, b_up, w_down, b_down)
    return _combine(x1, gates, dest_flat, p2d, row1(ple_norm), w_ple_gate.astype(BF16),
                    w_ple_proj.astype(BF16), ys)


def kernel(x, p, attn_norm, w_in, q_a_norm, w_uq, kv_a_norm, w_ukv, mla_q_nope_norm, mla_q_rope_norm,
           mla_k_nope_norm, mla_k_rope_norm, fox_q_norm, fox_k_norm, fox_f_bias, w_o_mla, w_o_fox, w_o,
           ffn_norm, w_router, b_router, w_up, b_up, w_down, b_down, ple_norm, w_ple_gate, w_ple_proj):
    batch, seq, d = x.shape
    depth = p.shape[0]
    cos, sin = _rope_tables(seq)
    x2d = x.reshape(batch * seq, d)
    for i in range(depth):
        x2d = _layer(x2d, p[i].reshape(batch * seq, -1), batch, seq, cos, sin,
                     attn_norm[i], w_in[i], q_a_norm[i], w_uq[i], kv_a_norm[i], w_ukv[i],
                     mla_q_nope_norm[i], mla_q_rope_norm[i], mla_k_nope_norm[i], mla_k_rope_norm[i],
                     fox_q_norm[i], fox_k_norm[i], fox_f_bias[i], w_o_mla[i], w_o_fox[i], w_o[i],
                     ffn_norm[i], w_router[i], b_router[i], w_up[i], b_up[i], w_down[i], b_down[i],
                     ple_norm[i], w_ple_gate[i], w_ple_proj[i])
    return x2d.reshape(batch, seq, d)
```

```python
import functools

import numpy as np
import jax
import jax.numpy as jnp
from jax import lax
from jax.experimental import pallas as pl
from jax.experimental.pallas import tpu as pltpu

F32 = jnp.float32
BF16 = jnp.bfloat16
I32 = jnp.int32

EPS = 1e-6
ROPE_THETA = 10000.0
MLA_HEADS = 8
MLA_NOPE = 128
MLA_ROPE = 64
MLA_V = 128
MLA_Q_LORA = 256
MLA_KV_LORA = 256
FOX_HEADS = 16
FOX_HD = 64
N_EXPERTS = 32
TOP_K = 4
SWIGLU_LIMIT = 7.0
SWIGLU_ALPHA = 1.702

LANES = 128
ZA_COLS = 640
FLOGIT_LANE = 64
FOX_DK = 128
V_PAD = 16
MOE_ROWS = 256
NEG_BIG = -1e30
LOG2E = 1.4426950408889634
VMEM_LIMIT = 56 * 1024 * 1024

TM_PROJ = 512
TM_DMA = 256
ATTN_TQ = 512


def _cparams(sem):
    return pltpu.CompilerParams(dimension_semantics=sem, vmem_limit_bytes=VMEM_LIMIT)


def _full(shape):
    nd = len(shape)
    return pl.BlockSpec(shape, lambda *_: (0,) * nd)


def _rms_rows(x, gain):
    return x * lax.rsqrt(jnp.mean(x * x, axis=-1, keepdims=True) + EPS) * gain


def _split3(x):
    hi = x.astype(BF16)
    r1 = x - hi.astype(F32)
    mid = r1.astype(BF16)
    lo = (r1 - mid.astype(F32)).astype(BF16)
    return hi, mid, lo


def _dot(a, b):
    return jnp.dot(a, b, preferred_element_type=F32)


def _inproj_kernel(x_ref, g_ref, wa_ref, wf_ref, wg_ref, za_ref, zf_ref, zg_ref):
    hb = _rms_rows(x_ref[...], g_ref[...]).astype(BF16)
    za_ref[...] = _dot(hb, wa_ref[...])
    zf_ref[...] = _dot(hb, wf_ref[...]).astype(BF16)
    zg_ref[...] = jax.nn.sigmoid(_dot(hb, wg_ref[...])).astype(BF16)


def _inproj(x2d, gain, wa, wf, wg):
    n, d = x2d.shape
    tm = TM_PROJ
    row = lambda c: pl.BlockSpec((tm, c), lambda i: (i, 0))
    return pl.pallas_call(
        _inproj_kernel,
        grid=(n // tm,),
        in_specs=[row(d), _full(gain.shape), _full(wa.shape), _full(wf.shape), _full(wg.shape)],
        out_specs=[row(wa.shape[1]), row(wf.shape[1]), row(wg.shape[1])],
        out_shape=[jax.ShapeDtypeStruct((n, wa.shape[1]), F32),
                   jax.ShapeDtypeStruct((n, wf.shape[1]), BF16),
                   jax.ShapeDtypeStruct((n, wg.shape[1]), BF16)],
        compiler_params=_cparams(("arbitrary",)),
        name="inproj",
    )(x2d, gain, wa, wf, wg)


def _group_rms(x, gmat, gmat_t, group, gain):
    ss = _dot((x * x).astype(BF16), gmat)
    inv = lax.rsqrt(ss * (1.0 / group) + EPS)
    inv_hi = inv.astype(BF16)
    inv_lo = (inv - inv_hi.astype(F32)).astype(BF16)
    inv_full = _dot(inv_hi, gmat_t) + _dot(inv_lo, gmat_t)
    return x * inv_full * gain


def _rot_half(x):
    c = x.shape[-1]
    lane = lax.broadcasted_iota(I32, x.shape, 1)
    first = (lane & 63) < 32
    return jnp.where(first, pltpu.roll(x, c - 32, 1), pltpu.roll(x, 32, 1))


def _log_sigmoid(x):
    return jnp.minimum(x, 0.0) - jnp.log1p(jnp.exp(-jnp.abs(x)))


def _prep_kernel(za_ref, zf_ref, cos_ref, sin_ref, wuq_ref, wukv_ref,
                 gqa_ref, gkva_ref, gqn_ref, gqr_ref, gkn_ref, gkr_ref, gfq_ref, gfk_ref, fb_ref,
                 g128_ref, g128t_ref, g64_ref, g64t_ref, tri_ref, eq_ref, ek_ref, oneq_ref, onek_ref,
                 qm_ref, km_ref, vm_ref, qf_ref, kf_ref, vf_ref, carry_ref):
    si = pl.program_id(1)
    za = za_ref[...]
    cos = cos_ref[...]
    sin = sin_ref[...]
    g128, g128t = g128_ref[...], g128t_ref[...]
    g64, g64t = g64_ref[...], g64t_ref[...]

    cq = _rms_rows(za[:, 0:MLA_Q_LORA], gqa_ref[...]).astype(BF16)
    ckv = _rms_rows(za[:, MLA_Q_LORA:MLA_Q_LORA + MLA_KV_LORA], gkva_ref[...]).astype(BF16)
    last = za[:, ZA_COLS - LANES:ZA_COLS]
    q = _dot(cq, wuq_ref[...])
    kv = _dot(ckv, wukv_ref[...])
    n_nope = MLA_HEADS * MLA_NOPE
    n_rope = MLA_HEADS * MLA_ROPE
    qn = _group_rms(q[:, :n_nope], g128, g128t, MLA_NOPE, gqn_ref[...])
    qr = _group_rms(q[:, n_nope:], g64[:n_rope], g64t[:, :n_rope], MLA_ROPE, gqr_ref[...])
    cos4 = jnp.concatenate([cos] * (n_rope // LANES), axis=-1)
    sin4 = jnp.concatenate([sin] * (n_rope // LANES), axis=-1)
    qr = qr * cos4 + _rot_half(qr) * sin4
    kn = _group_rms(kv[:, :n_nope], g128, g128t, MLA_NOPE, gkn_ref[...])
    lane = lax.broadcasted_iota(I32, last.shape, 1)
    kr_ss = jnp.sum(jnp.where(lane < MLA_ROPE, last * last, 0.0), axis=-1, keepdims=True)
    kr = last * lax.rsqrt(kr_ss * (1.0 / MLA_ROPE) + EPS) * gkr_ref[...]
    kr = kr * cos + _rot_half(kr) * sin
    kr_b = kr[:, :MLA_ROPE].astype(BF16)
    tm = za.shape[0]
    ones_rows = jnp.where(lax.broadcasted_iota(I32, (V_PAD, tm), 0) == 0, 1.0, 0.0).astype(BF16)
    qn_t = qn.T.astype(BF16)
    qr_t = qr.T.astype(BF16)
    v_t = kv[:, n_nope:].T.astype(BF16)
    for h in range(MLA_HEADS):
        qm_ref[0, h, 0:MLA_NOPE, :] = qn_t[h * MLA_NOPE:(h + 1) * MLA_NOPE, :]
        qm_ref[0, h, MLA_NOPE:MLA_NOPE + MLA_ROPE, :] = qr_t[h * MLA_ROPE:(h + 1) * MLA_ROPE, :]
        km_ref[0, h, :, 0:MLA_NOPE] = kn[:, h * MLA_NOPE:(h + 1) * MLA_NOPE].astype(BF16)
        km_ref[0, h, :, MLA_NOPE:MLA_NOPE + MLA_ROPE] = kr_b
        vm_ref[0, h, 0:MLA_V, :] = v_t[h * MLA_V:(h + 1) * MLA_V, :]
        vm_ref[0, h, MLA_V:MLA_V + V_PAD, :] = ones_rows

    nf = FOX_HEADS * FOX_HD
    zf = zf_ref[...]
    fq = _group_rms(zf[:, 0:nf].astype(F32), g64, g64t, FOX_HD, gfq_ref[...])
    fk = _group_rms(zf[:, nf:2 * nf].astype(F32), g64, g64t, FOX_HD, gfk_ref[...])
    fv = zf[:, 2 * nf:3 * nf]

    @pl.when(si == 0)
    def _():
        carry_ref[...] = jnp.zeros_like(carry_ref)

    logf = _log_sigmoid(last + fb_ref[...])
    l_hi, l_mid, l_lo = _split3(logf)
    tri = tri_ref[...]
    cum = carry_ref[...] + (_dot(tri, l_hi) + _dot(tri, l_mid) + _dot(tri, l_lo))
    tm = cum.shape[0]
    carry_ref[...] = cum[tm - 1:tm, :]
    fcat = jnp.concatenate(_split3(cum * LOG2E), axis=-1)
    augq = _dot(fcat, eq_ref[...]) + oneq_ref[...]
    augk = _dot(fcat, ek_ref[...]) + onek_ref[...]
    fq_t = fq.T.astype(BF16)
    augq_t = augq.T.astype(BF16)
    fv_t = fv.astype(F32).T.astype(BF16)
    for h in range(FOX_HEADS):
        sl = slice(h * FOX_HD, (h + 1) * FOX_HD)
        qf_ref[0, h, 0:FOX_HD, :] = fq_t[sl, :]
        qf_ref[0, h, FOX_HD:2 * FOX_HD, :] = augq_t[sl, :]
        kf_ref[0, h] = jnp.concatenate([fk[:, sl], augk[:, sl]], axis=-1).astype(BF16)
        vf_ref[0, h, 0:FOX_HD, :] = fv_t[sl, :]
        vf_ref[0, h, FOX_HD:FOX_HD + V_PAD, :] = ones_rows


def _group_matrix(cols, group):
    m = np.zeros((cols, LANES), np.float32)
    m[np.arange(cols), np.arange(cols) // group] = 1.0
    return m


def _bias_column_matrices():
    nf = FOX_HEADS * FOX_HD
    eq = np.zeros((3 * LANES, nf), np.float32)
    ek = np.zeros((3 * LANES, nf), np.float32)
    oneq = np.zeros((1, nf), np.float32)
    onek = np.zeros((1, nf), np.float32)
    for h in range(FOX_HEADS):
        for t in range(3):
            eq[t * LANES + FLOGIT_LANE + h, h * FOX_HD + t] = 1.0
            ek[t * LANES + FLOGIT_LANE + h, h * FOX_HD + 3 + t] = -1.0
            oneq[0, h * FOX_HD + 3 + t] = 1.0
            onek[0, h * FOX_HD + t] = 1.0
    return eq, ek, oneq, onek


def _prep(za, zf, cos, sin, wuq, wukv, gains, batch, seq):
    tm = TM_PROJ
    nt = seq // tm
    g128 = _group_matrix(MLA_HEADS * MLA_NOPE, MLA_NOPE)
    g64 = _group_matrix(FOX_HEADS * FOX_HD, FOX_HD)
    tri = np.tril(np.ones((tm, tm), np.float32))
    eq, ek, oneq, onek = _bias_column_matrices()
    consts = [jnp.asarray(g128, BF16), jnp.asarray(g128.T, BF16), jnp.asarray(g64, BF16), jnp.asarray(g64.T, BF16),
              jnp.asarray(tri, BF16), jnp.asarray(eq, BF16), jnp.asarray(ek, BF16),
              jnp.asarray(oneq, F32), jnp.asarray(onek, F32)]
    row = lambda c: pl.BlockSpec((tm, c), lambda b, s: (b * nt + s, 0))
    tab = pl.BlockSpec((tm, LANES), lambda b, s: (s, 0))
    head = lambda h, d: pl.BlockSpec((1, h, tm, d), lambda b, s: (b, 0, s, 0))
    head_t = lambda h, d: pl.BlockSpec((1, h, d, tm), lambda b, s: (b, 0, 0, s))
    hshape = lambda h, d: jax.ShapeDtypeStruct((batch, h, seq, d), BF16)
    hshape_t = lambda h, d: jax.ShapeDtypeStruct((batch, h, d, seq), BF16)
    dk = MLA_NOPE + MLA_ROPE
    return pl.pallas_call(
        _prep_kernel,
        grid=(batch, nt),
        in_specs=[row(za.shape[1]), row(zf.shape[1]), tab, tab, _full(wuq.shape), _full(wukv.shape)]
                 + [_full(g.shape) for g in gains] + [_full(c.shape) for c in consts],
        out_specs=[head_t(MLA_HEADS, dk), head(MLA_HEADS, dk), head_t(MLA_HEADS, MLA_V + V_PAD),
                   head_t(FOX_HEADS, FOX_DK), head(FOX_HEADS, FOX_DK), head_t(FOX_HEADS, FOX_HD + V_PAD)],
        out_shape=[hshape_t(MLA_HEADS, dk), hshape(MLA_HEADS, dk), hshape_t(MLA_HEADS, MLA_V + V_PAD),
                   hshape_t(FOX_HEADS, FOX_DK), hshape(FOX_HEADS, FOX_DK), hshape_t(FOX_HEADS, FOX_HD + V_PAD)],
        scratch_shapes=[pltpu.VMEM((1, LANES), F32)],
        compiler_params=_cparams(("arbitrary", "arbitrary")),
        name="prep",
    )(za, zf, cos, sin, wuq, wukv, *gains, *consts)


def _attn_kernel(qt_ref, k_ref, vt_ref, o_ref, *scratch, tq, tk, hpb, dv):
    qi = pl.program_id(2)
    m_refs, acc_refs = scratch[:hpb], scratch[hpb:]

    def kv_block(j, diagonal):
        start = pl.multiple_of(j * tk, tk)
        state = [(m_refs[hh][...], acc_refs[hh][...]) for hh in range(hpb)]

        def score(hh):
            return _dot(k_ref[0, hh, pl.ds(start, tk), :], qt_ref[0, hh])

        scores = {hh: score(hh) for hh in range(min(2, hpb))}
        results = []
        for hh in range(hpb):
            s, (m_old, acc_old) = scores.pop(hh), state[hh]
            if diagonal:
                key = lax.broadcasted_iota(I32, s.shape, 0)
                qry = lax.broadcasted_iota(I32, s.shape, 1)
                s = jnp.where(key <= qry, s, NEG_BIG)
            m_new = jnp.maximum(m_old, jnp.max(s, axis=0, keepdims=True))
            p = jnp.exp2(s - m_new).astype(BF16)
            pv = _dot(vt_ref[0, hh, :, pl.ds(start, tk)], p)
            results.append((m_new, acc_old * jnp.exp2(m_old - m_new) + pv))
            if hh + 2 < hpb:
                scores[hh + 2] = score(hh + 2)
        for hh in range(hpb):
            m_refs[hh][...] = results[hh][0]
            acc_refs[hh][...] = results[hh][1]

    for hh in range(hpb):
        m_refs[hh][...] = jnp.full(m_refs[hh].shape, NEG_BIG, F32)
        acc_refs[hh][...] = jnp.zeros(acc_refs[hh].shape, F32)

    def full_block(j, carry):
        kv_block(j, False)
        return carry

    lax.fori_loop(0, qi, full_block, 0)
    kv_block(qi, True)
    outs = []
    for hh in range(hpb):
        acc = acc_refs[hh][...]
        outs.append(acc[:dv, :] / acc[dv:dv + 1, :])
    o_ref[0] = jnp.concatenate(outs, axis=0).T.astype(o_ref.dtype)


def _attention(qt, k, vt, hpb, dv):
    b, h, s, dk = k.shape
    dvp = vt.shape[2]
    tq = tk = ATTN_TQ
    assert s % tq == 0 and h % hpb == 0 and (hpb * dv) % LANES == 0 and dvp == dv + V_PAD
    kern = functools.partial(_attn_kernel, tq=tq, tk=tk, hpb=hpb, dv=dv)
    return pl.pallas_call(
        kern,
        grid=(b, h // hpb, s // tq),
        in_specs=[pl.BlockSpec((1, hpb, dk, tq), lambda bi, hi, qi: (bi, hi, 0, qi)),
                  pl.BlockSpec((1, hpb, s, dk), lambda bi, hi, qi: (bi, hi, 0, 0), pipeline_mode=pl.Buffered(1)),
                  pl.BlockSpec((1, hpb, dvp, s), lambda bi, hi, qi: (bi, hi, 0, 0), pipeline_mode=pl.Buffered(1))],
        out_specs=pl.BlockSpec((1, tq, hpb * dv), lambda bi, hi, qi: (bi, qi, hi)),
        out_shape=jax.ShapeDtypeStruct((b, s, h * dv), BF16),
        scratch_shapes=[pltpu.VMEM((1, tq), F32)] * hpb + [pltpu.VMEM((dvp, tq), F32)] * hpb,
        compiler_params=_cparams(("arbitrary", "arbitrary", "arbitrary")),
        name="attention",
    )(qt, k, vt)


def _outproj_kernel(x_ref, om_ref, of_ref, zg_ref, wom_ref, wof_ref, wo_ref, gffn_ref, wr_ref, br_ref,
                    x1_ref, h2_ref, idx_ref, gate_ref):
    d = x_ref.shape[1]
    zg = zg_ref[...].astype(F32)
    mixed = zg[:, :d] * _dot(om_ref[...], wom_ref[...]) + zg[:, d:] * _dot(of_ref[...], wof_ref[...])
    x1 = x_ref[...] + _dot(mixed.astype(BF16), wo_ref[...])
    x1_ref[...] = x1
    h2 = _rms_rows(x1, gffn_ref[...])
    h2_ref[...] = h2
    logits = jnp.dot(h2, wr_ref[...], preferred_element_type=F32, precision=lax.Precision.HIGHEST) + br_ref[...]
    lane = lax.broadcasted_iota(I32, logits.shape, 1)
    lane_f = lane.astype(F32)
    work = jnp.where(lane < N_EXPERTS, logits, -jnp.inf)
    vals, idxs = [], []
    for _ in range(TOP_K):
        m = jnp.max(work, axis=-1, keepdims=True)
        idx = jnp.min(jnp.where(work == m, lane_f, float(LANES)), axis=-1, keepdims=True)
        vals.append(m)
        idxs.append(idx)
        work = jnp.where(lane_f == idx, -jnp.inf, work)
    exps = [jnp.exp(v - vals[0]) for v in vals]
    denom = exps[0] + exps[1] + exps[2] + exps[3]
    idx_out = jnp.zeros(logits.shape, F32)
    gate_out = jnp.zeros(logits.shape, F32)
    for kk in range(TOP_K):
        idx_out = jnp.where(lane == kk, idxs[kk], idx_out)
        gate_out = jnp.where(lane == kk, exps[kk] / denom, gate_out)
    idx_ref[...] = idx_out.astype(I32)
    gate_ref[...] = gate_out


def _outproj(x2d, o_mla, o_fox, zg, wom, wof, wo, gffn, wr, br):
    n, d = x2d.shape
    tm = TM_PROJ
    row = lambda c: pl.BlockSpec((tm, c), lambda i: (i, 0))
    return pl.pallas_call(
        _outproj_kernel,
        grid=(n // tm,),
        in_specs=[row(d), row(o_mla.shape[1]), row(o_fox.shape[1]), row(zg.shape[1]),
                  _full(wom.shape), _full(wof.shape), _full(wo.shape), _full(gffn.shape),
                  _full(wr.shape), _full(br.shape)],
        out_specs=[row(d), row(d), row(LANES), row(LANES)],
        out_shape=[jax.ShapeDtypeStruct((n, d), F32), jax.ShapeDtypeStruct((n, d), F32),
                   jax.ShapeDtypeStruct((n, LANES), I32), jax.ShapeDtypeStruct((n, LANES), F32)],
        compiler_params=_cparams(("arbitrary",)),
        name="outproj",
    )(x2d, o_mla, o_fox, zg, wom, wof, wo, gffn, wr, br)


def _route_kernel(idx_ref, lstrict_ref, ustrict_ref, dest_ref, be_ref, meta_ref, cnt_ref, start_ref, run_ref):
    phase = pl.program_id(0)
    i = pl.program_id(1)
    idx = idx_ref[...]
    lane = lax.broadcasted_iota(I32, idx.shape, 1)
    hit = [lane == idx[:, kk:kk + 1] for kk in range(TOP_K)]
    onehot = jnp.zeros(idx.shape, F32)
    for hk in hit:
        onehot = jnp.where(hk, 1.0, onehot)
    tile_cnt = jnp.sum(onehot, axis=0, keepdims=True)

    @pl.when((phase == 0) & (i == 0))
    def _():
        cnt_ref[...] = jnp.zeros_like(cnt_ref)

    @pl.when(phase == 0)
    def _():
        cnt_ref[...] += tile_cnt

    @pl.when((phase == 1) & (i == 0))
    def _():
        cnt = cnt_ref[...]
        nblk = jnp.floor((cnt + (MOE_ROWS - 1.0)) * (1.0 / MOE_ROWS))
        nblk8 = jnp.broadcast_to(nblk, (8, LANES)).astype(BF16)
        start = _dot(nblk8, ustrict_ref[...])[0:1, :]
        start_ref[...] = start
        run_ref[...] = jnp.zeros_like(run_ref)
        end = start + nblk
        nb = be_ref.shape[0]
        bid = lax.broadcasted_iota(I32, (nb, LANES), 0).astype(F32)
        lane_b = lax.broadcasted_iota(I32, (nb, LANES), 1)
        owned = jnp.where((lane_b < N_EXPERTS) & (end <= bid), 1.0, 0.0)
        be = jnp.minimum(jnp.sum(owned, axis=-1, keepdims=True), N_EXPERTS - 1.0)
        be_ref[...] = jnp.broadcast_to(be, (nb, LANES)).astype(I32)
        pad_lo = start * MOE_ROWS + cnt
        pad_hi = end * MOE_ROWS
        row8 = lax.broadcasted_iota(I32, (8, LANES), 0)
        meta = jnp.where(row8 == 0, pad_lo, jnp.where(row8 == 1, pad_hi, jnp.where(row8 == 2, end, 0.0)))
        meta_ref[...] = meta.astype(I32)

    @pl.when(phase == 1)
    def _():
        rank = _dot(lstrict_ref[...], onehot.astype(BF16))
        slot = start_ref[...] * MOE_ROWS + run_ref[...] + rank
        out = jnp.zeros(idx.shape, F32)
        for kk in range(TOP_K):
            dk = jnp.sum(jnp.where(hit[kk], slot, 0.0), axis=-1, keepdims=True)
            out = jnp.where(lane == kk, dk, out)
        dest_ref[...] = out.astype(I32)
        run_ref[...] += tile_cnt


def _route(idx, n_blocks):
    n = idx.shape[0]
    tm = TM_PROJ
    nt = n // tm
    nb = (n_blocks + 7) // 8 * 8
    lstrict = jnp.asarray(np.tril(np.ones((tm, tm), np.float32), -1), BF16)
    ustrict = jnp.asarray(np.triu(np.ones((LANES, LANES), np.float32), 1), BF16)
    return pl.pallas_call(
        _route_kernel,
        grid=(2, nt),
        in_specs=[pl.BlockSpec((tm, LANES), lambda p, i: (i, 0)), _full(lstrict.shape), _full(ustrict.shape)],
        out_specs=[pl.BlockSpec((tm, LANES), lambda p, i: (i * p, 0)), _full((nb, LANES)), _full((8, LANES))],
        out_shape=[jax.ShapeDtypeStruct((n, LANES), I32), jax.ShapeDtypeStruct((nb, LANES), I32),
                   jax.ShapeDtypeStruct((8, LANES), I32)],
        scratch_shapes=[pltpu.VMEM((1, LANES), F32), pltpu.VMEM((1, LANES), F32), pltpu.VMEM((1, LANES), F32)],
        compiler_params=_cparams(("arbitrary", "arbitrary")),
        name="route",
    )(idx, lstrict, ustrict)


def _dispatch_kernel(meta_ref, dest_ref, h_ref, xs_ref, zero_ref, sem, zsem):
    i = pl.program_id(0)
    tm = h_ref.shape[0]

    def row_copy(t, kk):
        d = dest_ref[t * TOP_K + kk]
        return pltpu.make_async_copy(h_ref.at[pl.ds(t, 1), :], xs_ref.at[pl.ds(d, 1), :], sem)

    def issue(t, carry):
        for kk in range(TOP_K):
            row_copy(t, kk).start()
        return carry

    def drain(t, carry):
        for kk in range(TOP_K):
            row_copy(t, kk).wait()
        return carry

    lax.fori_loop(0, tm, issue, 0)

    @pl.when(i == pl.num_programs(0) - 1)
    def _():
        zero_ref[...] = jnp.zeros_like(zero_ref)

        def pad_copy(r):
            return pltpu.make_async_copy(zero_ref.at[pl.ds(0, 1), :], xs_ref.at[pl.ds(r, 1), :], zsem)

        def blk_copy(b):
            return pltpu.make_async_copy(zero_ref, xs_ref.at[pl.ds(b * MOE_ROWS, MOE_ROWS), :], zsem)

        for e in range(N_EXPERTS):
            lo, hi = meta_ref[0, e], meta_ref[1, e]
            lax.fori_loop(lo, hi, lambda r, c: (pad_copy(r).start(), c)[1], 0)
            lax.fori_loop(lo, hi, lambda r, c: (pad_copy(r).wait(), c)[1], 0)
        n_used = meta_ref[2, N_EXPERTS - 1]
        n_blocks = xs_ref.shape[0] // MOE_ROWS
        lax.fori_loop(n_used, n_blocks, lambda b, c: (blk_copy(b).start(), c)[1], 0)
        lax.fori_loop(n_used, n_blocks, lambda b, c: (blk_copy(b).wait(), c)[1], 0)

    lax.fori_loop(0, tm, drain, 0)


def _dispatch(h2, dest_flat, meta, n_rows):
    n, d = h2.shape
    tm = TM_DMA
    grid_spec = pltpu.PrefetchScalarGridSpec(
        num_scalar_prefetch=1,
        grid=(n // tm,),
        in_specs=[pl.BlockSpec((tm * TOP_K,), lambda i, meta: (i,), memory_space=pltpu.SMEM),
                  pl.BlockSpec((tm, d), lambda i, meta: (i, 0))],
        out_specs=pl.BlockSpec(memory_space=pl.ANY),
        scratch_shapes=[pltpu.VMEM((MOE_ROWS, d), F32), pltpu.SemaphoreType.DMA(()), pltpu.SemaphoreType.DMA(())],
    )
    return pl.pallas_call(
        _dispatch_kernel,
        grid_spec=grid_spec,
        out_shape=jax.ShapeDtypeStruct((n_rows, d), F32),
        compiler_params=_cparams(("arbitrary",)),
        name="dispatch",
    )(meta, dest_flat, h2)


def _experts_kernel(be_ref, xs_ref, wu_ref, bu_ref, wd_ref, bd_ref, y_ref, wu_bf, wd_bf):
    b = pl.program_id(0)
    ff = wd_ref.shape[1]
    prev = be_ref[jnp.maximum(b - 1, 0)]

    @pl.when((b == 0) | (be_ref[b] != prev))
    def _():
        wu_bf[...] = wu_ref[0].astype(BF16)
        wd_bf[...] = wd_ref[0].astype(BF16)

    gu = _dot(xs_ref[...].astype(BF16), wu_bf[...]) + bu_ref[0]
    g = jnp.minimum(gu[:, :ff], SWIGLU_LIMIT)
    u = jnp.clip(gu[:, ff:], -SWIGLU_LIMIT, SWIGLU_LIMIT)
    act = g * jax.nn.sigmoid(SWIGLU_ALPHA * g) * (u + 1.0)
    y_ref[...] = _dot(act.astype(BF16), wd_bf[...]) + bd_ref[0]


def _experts(xs, block_expert, w_up, b_up, w_down, b_down):
    n_rows, d = xs.shape
    e, _, ff2 = w_up.shape
    ff = w_down.shape[1]
    grid_spec = pltpu.PrefetchScalarGridSpec(
        num_scalar_prefetch=1,
        grid=(n_rows // MOE_ROWS,),
        in_specs=[pl.BlockSpec((MOE_ROWS, d), lambda b, be: (b, 0)),
                  pl.BlockSpec((1, d, ff2), lambda b, be: (be[b], 0, 0)),
                  pl.BlockSpec((1, 1, ff2), lambda b, be: (be[b], 0, 0)),
                  pl.BlockSpec((1, ff, d), lambda b, be: (be[b], 0, 0)),
                  pl.BlockSpec((1, 1, d), lambda b, be: (be[b], 0, 0))],
        out_specs=pl.BlockSpec((MOE_ROWS, d), lambda b, be: (b, 0)),
        scratch_shapes=[pltpu.VMEM((d, ff2), BF16), pltpu.VMEM((ff, d), BF16)],
    )
    return pl.pallas_call(
        _experts_kernel,
        grid_spec=grid_spec,
        out_shape=jax.ShapeDtypeStruct((n_rows, d), F32),
        compiler_params=_cparams(("arbitrary",)),
        name="experts",
    )(block_expert, xs, w_up, b_up.reshape(e, 1, ff2), w_down, b_down.reshape(e, 1, d))


def _combine_kernel(dest_ref, x1_ref, gate_ref, p_ref, gple_ref, wpg_ref, wpp_ref, ys_ref, o_ref, rows_ref, sem):
    tm = x1_ref.shape[0]

    def row_copy(t, kk):
        d = dest_ref[t * TOP_K + kk]
        return pltpu.make_async_copy(ys_ref.at[pl.ds(d, 1), :], rows_ref.at[kk, pl.ds(t, 1), :], sem)

    def issue(t, carry):
        for kk in range(TOP_K):
            row_copy(t, kk).start()
        return carry

    def drain(t, carry):
        for kk in range(TOP_K):
            row_copy(t, kk).wait()
        return carry

    lax.fori_loop(0, tm, issue, 0)
    proj = _dot(p_ref[...].astype(BF16), wpp_ref[...])
    lax.fori_loop(0, tm, drain, 0)
    gates = gate_ref[...]
    x2 = x1_ref[...]
    for kk in range(TOP_K):
        x2 = x2 + gates[:, kk:kk + 1] * rows_ref[kk]
    hn = _rms_rows(x2, gple_ref[...]).astype(BF16)
    o_ref[...] = x2 + jax.nn.sigmoid(_dot(hn, wpg_ref[...])) * proj


def _combine(x1, gates, dest_flat, p2d, gple, wpg, wpp, ys):
    n, d = x1.shape
    tm = TM_DMA
    row = lambda c: pl.BlockSpec((tm, c), lambda i: (i, 0))
    return pl.pallas_call(
        _combine_kernel,
        grid=(n // tm,),
        in_specs=[pl.BlockSpec((tm * TOP_K,), lambda i: (i,), memory_space=pltpu.SMEM),
                  row(d), row(LANES), row(p2d.shape[1]), _full(gple.shape), _full(wpg.shape), _full(wpp.shape),
                  pl.BlockSpec(memory_space=pl.ANY)],
        out_specs=row(d),
        out_shape=jax.ShapeDtypeStruct((n, d), F32),
        scratch_shapes=[pltpu.VMEM((TOP_K, tm, d), F32), pltpu.SemaphoreType.DMA(())],
        compiler_params=_cparams(("arbitrary",)),
        name="combine",
    )(dest_flat, x1, gates, p2d, gple, wpg, wpp, ys)


def _rope_tables(seq):
    pos = jnp.arange(seq, dtype=F32)
    inv_freq = ROPE_THETA ** (-jnp.arange(0, MLA_ROPE, 2, dtype=F32) / MLA_ROPE)
    ang = pos[:, None] * inv_freq[None, :]
    c, s = jnp.cos(ang), jnp.sin(ang)
    reps = LANES // MLA_ROPE
    return jnp.tile(jnp.concatenate([c, c], -1), (1, reps)), jnp.tile(jnp.concatenate([-s, s], -1), (1, reps))


def _layer(x2d, p2d, batch, seq, cos, sin, attn_norm, w_in, q_a_norm, w_uq, kv_a_norm, w_ukv,
           mla_q_nope_norm, mla_q_rope_norm, mla_k_nope_norm, mla_k_rope_norm,
           fox_q_norm, fox_k_norm, fox_f_bias, w_o_mla, w_o_fox, w_o,
           ffn_norm, w_router, b_router, w_up, b_up, w_down, b_down, ple_norm, w_ple_gate, w_ple_proj):
    n, d = x2d.shape
    nf = FOX_HEADS * FOX_HD
    row1 = lambda v: v.reshape(1, -1).astype(F32)

    o = np.cumsum([0, MLA_Q_LORA, MLA_KV_LORA, MLA_ROPE, nf, nf, nf, FOX_HEADS, d, d])
    wa = jnp.concatenate([w_in[:, o[0]:o[3]], w_in[:, o[6]:o[7]],
                          jnp.zeros((d, ZA_COLS - int(o[3]) - FOX_HEADS), w_in.dtype)], axis=1).astype(BF16)
    wf = w_in[:, o[3]:o[6]].astype(BF16)
    wg = w_in[:, o[7]:o[9]].astype(BF16)
    za, zf, zg = _inproj(x2d, row1(attn_norm), wa, wf, wg)

    wq3 = w_uq.reshape(MLA_Q_LORA, MLA_HEADS, MLA_NOPE + MLA_ROPE)
    wuq = jnp.concatenate([wq3[:, :, :MLA_NOPE].reshape(MLA_Q_LORA, -1),
                           wq3[:, :, MLA_NOPE:].reshape(MLA_Q_LORA, -1)], axis=1).astype(BF16)
    wkv3 = w_ukv.reshape(MLA_KV_LORA, MLA_HEADS, MLA_NOPE + MLA_V)
    wukv = jnp.concatenate([wkv3[:, :, :MLA_NOPE].reshape(MLA_KV_LORA, -1),
                            wkv3[:, :, MLA_NOPE:].reshape(MLA_KV_LORA, -1)], axis=1).astype(BF16)
    mla_scale = (MLA_NOPE + MLA_ROPE) ** -0.5 * LOG2E
    fox_scale = FOX_HD ** -0.5 * LOG2E
    gkr = jnp.concatenate([mla_k_rope_norm.astype(F32), jnp.zeros((LANES - MLA_ROPE,), F32)])
    fb = jnp.zeros((LANES,), F32).at[FLOGIT_LANE:FLOGIT_LANE + FOX_HEADS].set(fox_f_bias.astype(F32))
    gains = [row1(q_a_norm), row1(kv_a_norm),
             row1(jnp.tile(mla_q_nope_norm, MLA_HEADS) * mla_scale),
             row1(jnp.tile(mla_q_rope_norm, MLA_HEADS) * mla_scale),
             row1(jnp.tile(mla_k_nope_norm, MLA_HEADS)), row1(gkr),
             row1(jnp.tile(fox_q_norm, FOX_HEADS) * fox_scale), row1(jnp.tile(fox_k_norm, FOX_HEADS)), row1(fb)]
    qm, km, vm, qf, kf, vf = _prep(za, zf, cos, sin, wuq, wukv, gains, batch, seq)

    o_mla = _attention(qm, km, vm, hpb=4, dv=MLA_V).reshape(n, MLA_HEADS * MLA_V)
    o_fox = _attention(qf, kf, vf, hpb=4, dv=FOX_HD).reshape(n, nf)

    wr = jnp.concatenate([w_router.astype(F32), jnp.zeros((d, LANES - N_EXPERTS), F32)], axis=1)
    br = jnp.concatenate([b_router.astype(F32), jnp.zeros((LANES - N_EXPERTS,), F32)]).reshape(1, LANES)
    x1, h2, idx, gates = _outproj(x2d, o_mla, o_fox, zg, w_o_mla.astype(BF16), w_o_fox.astype(BF16),
                                  w_o.astype(BF16), row1(ffn_norm), wr, br)

    n_blocks = n * TOP_K // MOE_ROWS + N_EXPERTS
    dest, be, meta = _route(idx, n_blocks)
    dest_flat = dest[:, :TOP_K].reshape(-1)
    xs = _dispatch(h2, dest_flat, meta, n_blocks * MOE_ROWS)
    ys = _experts(xs, be[:n_blocks, 0], w_up, b_up, w_down, b_down)
    return _combine(x1, gates, dest_flat, p2d, row1(ple_norm), w_ple_gate.astype(BF16),
                    w_ple_proj.astype(BF16), ys)


def kernel(x, p, attn_norm, w_in, q_a_norm, w_uq, kv_a_norm, w_ukv, mla_q_nope_norm, mla_q_rope_norm,
           mla_k_nope_norm, mla_k_rope_norm, fox_q_norm, fox_k_norm, fox_f_bias, w_o_mla, w_o_fox, w_o,
           ffn_norm, w_router, b_router, w_up, b_up, w_down, b_down, ple_norm, w_ple_gate, w_ple_proj):
    batch, seq, d = x.shape
    depth = p.shape[0]
    cos, sin = _rope_tables(seq)
    x2d = x.reshape(batch * seq, d)
    for i in range(depth):
        x2d = _layer(x2d, p[i].reshape(batch * seq, -1), batch, seq, cos, sin,
                     attn_norm[i], w_in[i], q_a_norm[i], w_uq[i], kv_a_norm[i], w_ukv[i],
                     mla_q_nope_norm[i], mla_q_rope_norm[i], mla_k_nope_norm[i], mla_k_rope_norm[i],
                     fox_q_norm[i], fox_k_norm[i], fox_f_bias[i], w_o_mla[i], w_o_fox[i], w_o[i],
                     ffn_norm[i], w_router[i], b_router[i], w_up[i], b_up[i], w_down[i], b_down[i],
                     ple_norm[i], w_ple_gate[i], w_ple_proj[i])
    return x2d.reshape(batch, seq, d)
```

```python
import functools

import numpy as np
import jax
import jax.numpy as jnp
from jax import lax
from jax.experimental import pallas as pl
from jax.experimental.pallas import tpu as pltpu

F32 = jnp.float32
BF16 = jnp.bfloat16
I32 = jnp.int32

EPS = 1e-6
ROPE_THETA = 10000.0
MLA_HEADS = 8
MLA_NOPE = 128
MLA_ROPE = 64
MLA_V = 128
MLA_Q_LORA = 256
MLA_KV_LORA = 256
FOX_HEADS = 16
FOX_HD = 64
N_EXPERTS = 32
TOP_K = 4
SWIGLU_LIMIT = 7.0
SWIGLU_ALPHA = 1.702

LANES = 128
ZA_COLS = 640
FLOGIT_LANE = 64
FOX_DK = 128
V_PAD = 16
MOE_ROWS = 256
NEG_BIG = -1e30
LOG2E = 1.4426950408889634
VMEM_LIMIT = 56 * 1024 * 1024

TM_PROJ = 512
TM_DMA = 256
ATTN_TQ = 512


def _cparams(sem):
    return pltpu.CompilerParams(dimension_semantics=sem, vmem_limit_bytes=VMEM_LIMIT)


def _full(shape):
    nd = len(shape)
    return pl.BlockSpec(shape, lambda *_: (0,) * nd)


def _rms_rows(x, gain):
    return x * lax.rsqrt(jnp.mean(x * x, axis=-1, keepdims=True) + EPS) * gain


def _split3(x):
    hi = x.astype(BF16)
    r1 = x - hi.astype(F32)
    mid = r1.astype(BF16)
    lo = (r1 - mid.astype(F32)).astype(BF16)
    return hi, mid, lo


def _dot(a, b):
    return jnp.dot(a, b, preferred_element_type=F32)


def _inproj_kernel(x_ref, g_ref, wa_ref, wf_ref, wg_ref, za_ref, zf_ref, zg_ref):
    hb = _rms_rows(x_ref[...], g_ref[...]).astype(BF16)
    za_ref[...] = _dot(hb, wa_ref[...])
    zf_ref[...] = _dot(hb, wf_ref[...]).astype(BF16)
    zg_ref[...] = jax.nn.sigmoid(_dot(hb, wg_ref[...])).astype(BF16)


def _inproj(x2d, gain, wa, wf, wg):
    n, d = x2d.shape
    tm = TM_PROJ
    row = lambda c: pl.BlockSpec((tm, c), lambda i: (i, 0))
    return pl.pallas_call(
        _inproj_kernel,
        grid=(n // tm,),
        in_specs=[row(d), _full(gain.shape), _full(wa.shape), _full(wf.shape), _full(wg.shape)],
        out_specs=[row(wa.shape[1]), row(wf.shape[1]), row(wg.shape[1])],
        out_shape=[jax.ShapeDtypeStruct((n, wa.shape[1]), F32),
                   jax.ShapeDtypeStruct((n, wf.shape[1]), BF16),
                   jax.ShapeDtypeStruct((n, wg.shape[1]), BF16)],
        compiler_params=_cparams(("arbitrary",)),
        name="inproj",
    )(x2d, gain, wa, wf, wg)


def _group_rms(x, gmat, gmat_t, group, gain):
    ss = _dot((x * x).astype(BF16), gmat)
    inv = lax.rsqrt(ss * (1.0 / group) + EPS)
    inv_hi = inv.astype(BF16)
    inv_lo = (inv - inv_hi.astype(F32)).astype(BF16)
    inv_full = _dot(inv_hi, gmat_t) + _dot(inv_lo, gmat_t)
    return x * inv_full * gain


def _rot_half(x):
    c = x.shape[-1]
    lane = lax.broadcasted_iota(I32, x.shape, 1)
    first = (lane & 63) < 32
    return jnp.where(first, pltpu.roll(x, c - 32, 1), pltpu.roll(x, 32, 1))


def _log_sigmoid(x):
    return jnp.minimum(x, 0.0) - jnp.log1p(jnp.exp(-jnp.abs(x)))


def _prep_kernel(za_ref, zf_ref, cos_ref, sin_ref, wuq_ref, wukv_ref,
                 gqa_ref, gkva_ref, gqn_ref, gqr_ref, gkn_ref, gkr_ref, gfq_ref, gfk_ref, fb_ref,
                 g128_ref, g128t_ref, g64_ref, g64t_ref, tri_ref, eq_ref, ek_ref, oneq_ref, onek_ref,
                 qm_ref, km_ref, vm_ref, qf_ref, kf_ref, vf_ref, carry_ref):
    si = pl.program_id(1)
    za = za_ref[...]
    cos = cos_ref[...]
    sin = sin_ref[...]
    g128, g128t = g128_ref[...], g128t_ref[...]
    g64, g64t = g64_ref[...], g64t_ref[...]

    cq = _rms_rows(za[:, 0:MLA_Q_LORA], gqa_ref[...]).astype(BF16)
    ckv = _rms_rows(za[:, MLA_Q_LORA:MLA_Q_LORA + MLA_KV_LORA], gkva_ref[...]).astype(BF16)
    last = za[:, ZA_COLS - LANES:ZA_COLS]
    q = _dot(cq, wuq_ref[...])
    kv = _dot(ckv, wukv_ref[...])
    n_nope = MLA_HEADS * MLA_NOPE
    n_rope = MLA_HEADS * MLA_ROPE
    qn = _group_rms(q[:, :n_nope], g128, g128t, MLA_NOPE, gqn_ref[...])
    qr = _group_rms(q[:, n_nope:], g64[:n_rope], g64t[:, :n_rope], MLA_ROPE, gqr_ref[...])
    cos4 = jnp.concatenate([cos] * (n_rope // LANES), axis=-1)
    sin4 = jnp.concatenate([sin] * (n_rope // LANES), axis=-1)
    qr = qr * cos4 + _rot_half(qr) * sin4
    kn = _group_rms(kv[:, :n_nope], g128, g128t, MLA_NOPE, gkn_ref[...])
    lane = lax.broadcasted_iota(I32, last.shape, 1)
    kr_ss = jnp.sum(jnp.where(lane < MLA_ROPE, last * last, 0.0), axis=-1, keepdims=True)
    kr = last * lax.rsqrt(kr_ss * (1.0 / MLA_ROPE) + EPS) * gkr_ref[...]
    kr = kr * cos + _rot_half(kr) * sin
    kr_b = kr[:, :MLA_ROPE].astype(BF16)
    tm = za.shape[0]
    ones_rows = jnp.where(lax.broadcasted_iota(I32, (V_PAD, tm), 0) == 0, 1.0, 0.0).astype(BF16)
    qn_t = qn.T.astype(BF16)
    qr_t = qr.T.astype(BF16)
    v_t = kv[:, n_nope:].T.astype(BF16)
    for h in range(MLA_HEADS):
        qm_ref[0, h, 0:MLA_NOPE, :] = qn_t[h * MLA_NOPE:(h + 1) * MLA_NOPE, :]
        qm_ref[0, h, MLA_NOPE:MLA_NOPE + MLA_ROPE, :] = qr_t[h * MLA_ROPE:(h + 1) * MLA_ROPE, :]
        km_ref[0, h, :, 0:MLA_NOPE] = kn[:, h * MLA_NOPE:(h + 1) * MLA_NOPE].astype(BF16)
        km_ref[0, h, :, MLA_NOPE:MLA_NOPE + MLA_ROPE] = kr_b
        vm_ref[0, h, 0:MLA_V, :] = v_t[h * MLA_V:(h + 1) * MLA_V, :]
        vm_ref[0, h, MLA_V:MLA_V + V_PAD, :] = ones_rows

    nf = FOX_HEADS * FOX_HD
    zf = zf_ref[...]
    fq = _group_rms(zf[:, 0:nf].astype(F32), g64, g64t, FOX_HD, gfq_ref[...])
    fk = _group_rms(zf[:, nf:2 * nf].astype(F32), g64, g64t, FOX_HD, gfk_ref[...])
    fv = zf[:, 2 * nf:3 * nf]

    @pl.when(si == 0)
    def _():
        carry_ref[...] = jnp.zeros_like(carry_ref)

    logf = _log_sigmoid(last + fb_ref[...])
    l_hi, l_mid, l_lo = _split3(logf)
    tri = tri_ref[...]
    cum = carry_ref[...] + (_dot(tri, l_hi) + _dot(tri, l_mid) + _dot(tri, l_lo))
    tm = cum.shape[0]
    carry_ref[...] = cum[tm - 1:tm, :]
    fcat = jnp.concatenate(_split3(cum * LOG2E), axis=-1)
    augq = _dot(fcat, eq_ref[...]) + oneq_ref[...]
    augk = _dot(fcat, ek_ref[...]) + onek_ref[...]
    fq_t = fq.T.astype(BF16)
    augq_t = augq.T.astype(BF16)
    fv_t = fv.astype(F32).T.astype(BF16)
    for h in range(FOX_HEADS):
        sl = slice(h * FOX_HD, (h + 1) * FOX_HD)
        qf_ref[0, h, 0:FOX_HD, :] = fq_t[sl, :]
        qf_ref[0, h, FOX_HD:2 * FOX_HD, :] = augq_t[sl, :]
        kf_ref[0, h] = jnp.concatenate([fk[:, sl], augk[:, sl]], axis=-1).astype(BF16)
        vf_ref[0, h, 0:FOX_HD, :] = fv_t[sl, :]
        vf_ref[0, h, FOX_HD:FOX_HD + V_PAD, :] = ones_rows


def _group_matrix(cols, group):
    m = np.zeros((cols, LANES), np.float32)
    m[np.arange(cols), np.arange(cols) // group] = 1.0
    return m


def _bias_column_matrices():
    nf = FOX_HEADS * FOX_HD
    eq = np.zeros((3 * LANES, nf), np.float32)
    ek = np.zeros((3 * LANES, nf), np.float32)
    oneq = np.zeros((1, nf), np.float32)
    onek = np.zeros((1, nf), np.float32)
    for h in range(FOX_HEADS):
        for t in range(3):
            eq[t * LANES + FLOGIT_LANE + h, h * FOX_HD + t] = 1.0
            ek[t * LANES + FLOGIT_LANE + h, h * FOX_HD + 3 + t] = -1.0
            oneq[0, h * FOX_HD + 3 + t] = 1.0
            onek[0, h * FOX_HD + t] = 1.0
    return eq, ek, oneq, onek


def _prep(za, zf, cos, sin, wuq, wukv, gains, batch, seq):
    tm = TM_PROJ
    nt = seq // tm
    g128 = _group_matrix(MLA_HEADS * MLA_NOPE, MLA_NOPE)
    g64 = _group_matrix(FOX_HEADS * FOX_HD, FOX_HD)
    tri = np.tril(np.ones((tm, tm), np.float32))
    eq, ek, oneq, onek = _bias_column_matrices()
    consts = [jnp.asarray(g128, BF16), jnp.asarray(g128.T, BF16), jnp.asarray(g64, BF16), jnp.asarray(g64.T, BF16),
              jnp.asarray(tri, BF16), jnp.asarray(eq, BF16), jnp.asarray(ek, BF16),
              jnp.asarray(oneq, F32), jnp.asarray(onek, F32)]
    row = lambda c: pl.BlockSpec((tm, c), lambda b, s: (b * nt + s, 0))
    tab = pl.BlockSpec((tm, LANES), lambda b, s: (s, 0))
    head = lambda h, d: pl.BlockSpec((1, h, tm, d), lambda b, s: (b, 0, s, 0))
    head_t = lambda h, d: pl.BlockSpec((1, h, d, tm), lambda b, s: (b, 0, 0, s))
    hshape = lambda h, d: jax.ShapeDtypeStruct((batch, h, seq, d), BF16)
    hshape_t = lambda h, d: jax.ShapeDtypeStruct((batch, h, d, seq), BF16)
    dk = MLA_NOPE + MLA_ROPE
    return pl.pallas_call(
        _prep_kernel,
        grid=(batch, nt),
        in_specs=[row(za.shape[1]), row(zf.shape[1]), tab, tab, _full(wuq.shape), _full(wukv.shape)]
                 + [_full(g.shape) for g in gains] + [_full(c.shape) for c in consts],
        out_specs=[head_t(MLA_HEADS, dk), head(MLA_HEADS, dk), head_t(MLA_HEADS, MLA_V + V_PAD),
                   head_t(FOX_HEADS, FOX_DK), head(FOX_HEADS, FOX_DK), head_t(FOX_HEADS, FOX_HD + V_PAD)],
        out_shape=[hshape_t(MLA_HEADS, dk), hshape(MLA_HEADS, dk), hshape_t(MLA_HEADS, MLA_V + V_PAD),
                   hshape_t(FOX_HEADS, FOX_DK), hshape(FOX_HEADS, FOX_DK), hshape_t(FOX_HEADS, FOX_HD + V_PAD)],
        scratch_shapes=[pltpu.VMEM((1, LANES), F32)],
        compiler_params=_cparams(("arbitrary", "arbitrary")),
        name="prep",
    )(za, zf, cos, sin, wuq, wukv, *gains, *consts)


def _attn_kernel(qt_ref, k_ref, vt_ref, o_ref, *scratch, tq, tk, hpb, dv):
    qi = pl.program_id(2)
    m_refs, acc_refs, s_refs = scratch[:hpb], scratch[hpb:2 * hpb], scratch[2 * hpb:]
    n_ahead = len(s_refs)

    def score(hh, j):
        return _dot(k_ref[0, hh, pl.ds(pl.multiple_of(j * tk, tk), tk), :], qt_ref[0, hh])

    def kv_block(j, diagonal):
        start = pl.multiple_of(j * tk, tk)
        state = [(m_refs[hh][...], acc_refs[hh][...]) for hh in range(hpb)]
        ready = {hh: s_refs[hh][...] for hh in range(n_ahead)}
        upcoming = list(range(n_ahead, hpb)) + ([] if diagonal else [("next", hh) for hh in range(n_ahead)])

        def issue_next_score():
            if not upcoming:
                return
            item = upcoming.pop(0)
            if isinstance(item, tuple):
                s_refs[item[1]][...] = score(item[1], j + 1)
            else:
                ready[item] = score(item, j)

        issue_next_score()
        results = []
        for hh in range(hpb):
            s, (m_old, acc_old) = ready.pop(hh), state[hh]
            if diagonal:
                key = lax.broadcasted_iota(I32, s.shape, 0)
                qry = lax.broadcasted_iota(I32, s.shape, 1)
                s = jnp.where(key <= qry, s, NEG_BIG)
            m_new = jnp.maximum(m_old, jnp.max(s, axis=0, keepdims=True))
            p = jnp.exp2((s - m_new).astype(BF16))
            pv = _dot(vt_ref[0, hh, :, pl.ds(start, tk)], p)
            results.append((m_new, acc_old * jnp.exp2(m_old - m_new) + pv))
            issue_next_score()
        while upcoming:
            issue_next_score()
        for hh in range(hpb):
            m_refs[hh][...] = results[hh][0]
            acc_refs[hh][...] = results[hh][1]

    for hh in range(hpb):
        m_refs[hh][...] = jnp.full(m_refs[hh].shape, NEG_BIG, F32)
        acc_refs[hh][...] = jnp.zeros(acc_refs[hh].shape, F32)
    for hh in range(n_ahead):
        s_refs[hh][...] = score(hh, 0)

    def full_block(j, carry):
        kv_block(j, False)
        return carry

    lax.fori_loop(0, qi, full_block, 0)
    kv_block(qi, True)
    outs = []
    for hh in range(hpb):
        acc = acc_refs[hh][...]
        outs.append(acc[:dv, :] / acc[dv:dv + 1, :])
    o_ref[0] = jnp.concatenate(outs, axis=0).T.astype(o_ref.dtype)


def _attention(qt, k, vt, hpb, dv):
    b, h, s, dk = k.shape
    dvp = vt.shape[2]
    tq = tk = ATTN_TQ
    assert s % tq == 0 and h % hpb == 0 and (hpb * dv) % LANES == 0 and dvp == dv + V_PAD
    kern = functools.partial(_attn_kernel, tq=tq, tk=tk, hpb=hpb, dv=dv)
    kv_bytes = hpb * s * (-(-dk // LANES) * LANES + dvp) * 2
    kv_mode = {} if 2 * kv_bytes <= VMEM_LIMIT // 2 else {"pipeline_mode": pl.Buffered(1)}
    return pl.pallas_call(
        kern,
        grid=(b, h // hpb, s // tq),
        in_specs=[pl.BlockSpec((1, hpb, dk, tq), lambda bi, hi, qi: (bi, hi, 0, qi)),
                  pl.BlockSpec((1, hpb, s, dk), lambda bi, hi, qi: (bi, hi, 0, 0), **kv_mode),
                  pl.BlockSpec((1, hpb, dvp, s), lambda bi, hi, qi: (bi, hi, 0, 0), **kv_mode)],
        out_specs=pl.BlockSpec((1, tq, hpb * dv), lambda bi, hi, qi: (bi, qi, hi)),
        out_shape=jax.ShapeDtypeStruct((b, s, h * dv), BF16),
        scratch_shapes=([pltpu.VMEM((1, tq), F32)] * hpb + [pltpu.VMEM((dvp, tq), F32)] * hpb
                        + [pltpu.VMEM((tk, tq), F32)] * min(2, hpb)),
        compiler_params=_cparams(("arbitrary", "arbitrary", "arbitrary")),
        name="attention",
    )(qt, k, vt)


def _outproj_kernel(x_ref, om_ref, of_ref, zg_ref, wom_ref, wof_ref, wo_ref, gffn_ref, wr_ref, br_ref,
                    x1_ref, h2_ref, idx_ref, gate_ref):
    d = x_ref.shape[1]
    zg = zg_ref[...].astype(F32)
    mixed = zg[:, :d] * _dot(om_ref[...], wom_ref[...]) + zg[:, d:] * _dot(of_ref[...], wof_ref[...])
    x1 = x_ref[...] + _dot(mixed.astype(BF16), wo_ref[...])
    x1_ref[...] = x1
    h2 = _rms_rows(x1, gffn_ref[...])
    h2_ref[...] = h2
    logits = jnp.dot(h2, wr_ref[...], preferred_element_type=F32, precision=lax.Precision.HIGHEST) + br_ref[...]
    lane = lax.broadcasted_iota(I32, logits.shape, 1)
    lane_f = lane.astype(F32)
    work = jnp.where(lane < N_EXPERTS, logits, -jnp.inf)
    vals, idxs = [], []
    for _ in range(TOP_K):
        m = jnp.max(work, axis=-1, keepdims=True)
        idx = jnp.min(jnp.where(work == m, lane_f, float(LANES)), axis=-1, keepdims=True)
        vals.append(m)
        idxs.append(idx)
        work = jnp.where(lane_f == idx, -jnp.inf, work)
    exps = [jnp.exp(v - vals[0]) for v in vals]
    denom = exps[0] + exps[1] + exps[2] + exps[3]
    idx_out = jnp.zeros(logits.shape, F32)
    gate_out = jnp.zeros(logits.shape, F32)
    for kk in range(TOP_K):
        idx_out = jnp.where(lane == kk, idxs[kk], idx_out)
        gate_out = jnp.where(lane == kk, exps[kk] / denom, gate_out)
    idx_ref[...] = idx_out.astype(I32)
    gate_ref[...] = gate_out


def _outproj(x2d, o_mla, o_fox, zg, wom, wof, wo, gffn, wr, br):
    n, d = x2d.shape
    tm = TM_PROJ
    row = lambda c: pl.BlockSpec((tm, c), lambda i: (i, 0))
    return pl.pallas_call(
        _outproj_kernel,
        grid=(n // tm,),
        in_specs=[row(d), row(o_mla.shape[1]), row(o_fox.shape[1]), row(zg.shape[1]),
                  _full(wom.shape), _full(wof.shape), _full(wo.shape), _full(gffn.shape),
                  _full(wr.shape), _full(br.shape)],
        out_specs=[row(d), row(d), row(LANES), row(LANES)],
        out_shape=[jax.ShapeDtypeStruct((n, d), F32), jax.ShapeDtypeStruct((n, d), F32),
                   jax.ShapeDtypeStruct((n, LANES), I32), jax.ShapeDtypeStruct((n, LANES), F32)],
        compiler_params=_cparams(("arbitrary",)),
        name="outproj",
    )(x2d, o_mla, o_fox, zg, wom, wof, wo, gffn, wr, br)


def _route_kernel(idx_ref, lstrict_ref, ustrict_ref, dest_ref, be_ref, meta_ref, cnt_ref, start_ref, run_ref):
    phase = pl.program_id(0)
    i = pl.program_id(1)
    idx = idx_ref[...]
    lane = lax.broadcasted_iota(I32, idx.shape, 1)
    hit = [lane == idx[:, kk:kk + 1] for kk in range(TOP_K)]
    onehot = jnp.zeros(idx.shape, F32)
    for hk in hit:
        onehot = jnp.where(hk, 1.0, onehot)
    tile_cnt = jnp.sum(onehot, axis=0, keepdims=True)

    @pl.when((phase == 0) & (i == 0))
    def _():
        cnt_ref[...] = jnp.zeros_like(cnt_ref)

    @pl.when(phase == 0)
    def _():
        cnt_ref[...] += tile_cnt

    @pl.when((phase == 1) & (i == 0))
    def _():
        cnt = cnt_ref[...]
        nblk = jnp.floor((cnt + (MOE_ROWS - 1.0)) * (1.0 / MOE_ROWS))
        nblk8 = jnp.broadcast_to(nblk, (8, LANES)).astype(BF16)
        start = _dot(nblk8, ustrict_ref[...])[0:1, :]
        start_ref[...] = start
        run_ref[...] = jnp.zeros_like(run_ref)
        end = start + nblk
        nb = be_ref.shape[0]
        bid = lax.broadcasted_iota(I32, (nb, LANES), 0).astype(F32)
        lane_b = lax.broadcasted_iota(I32, (nb, LANES), 1)
        owned = jnp.where((lane_b < N_EXPERTS) & (end <= bid), 1.0, 0.0)
        be = jnp.minimum(jnp.sum(owned, axis=-1, keepdims=True), N_EXPERTS - 1.0)
        be_ref[...] = jnp.broadcast_to(be, (nb, LANES)).astype(I32)
        pad_lo = start * MOE_ROWS + cnt
        pad_hi = end * MOE_ROWS
        row8 = lax.broadcasted_iota(I32, (8, LANES), 0)
        meta = jnp.where(row8 == 0, pad_lo, jnp.where(row8 == 1, pad_hi, jnp.where(row8 == 2, end, 0.0)))
        meta_ref[...] = meta.astype(I32)

    @pl.when(phase == 1)
    def _():
        rank = _dot(lstrict_ref[...], onehot.astype(BF16))
        slot = start_ref[...] * MOE_ROWS + run_ref[...] + rank
        out = jnp.zeros(idx.shape, F32)
        for kk in range(TOP_K):
            dk = jnp.sum(jnp.where(hit[kk], slot, 0.0), axis=-1, keepdims=True)
            out = jnp.where(lane == kk, dk, out)
        dest_ref[...] = out.astype(I32)
        run_ref[...] += tile_cnt


def _route(idx, n_blocks):
    n = idx.shape[0]
    tm = TM_PROJ
    nt = n // tm
    nb = (n_blocks + 7) // 8 * 8
    lstrict = jnp.asarray(np.tril(np.ones((tm, tm), np.float32), -1), BF16)
    ustrict = jnp.asarray(np.triu(np.ones((LANES, LANES), np.float32), 1), BF16)
    return pl.pallas_call(
        _route_kernel,
        grid=(2, nt),
        in_specs=[pl.BlockSpec((tm, LANES), lambda p, i: (i, 0)), _full(lstrict.shape), _full(ustrict.shape)],
        out_specs=[pl.BlockSpec((tm, LANES), lambda p, i: (i * p, 0)), _full((nb, LANES)), _full((8, LANES))],
        out_shape=[jax.ShapeDtypeStruct((n, LANES), I32), jax.ShapeDtypeStruct((nb, LANES), I32),
                   jax.ShapeDtypeStruct((8, LANES), I32)],
        scratch_shapes=[pltpu.VMEM((1, LANES), F32), pltpu.VMEM((1, LANES), F32), pltpu.VMEM((1, LANES), F32)],
        compiler_params=_cparams(("arbitrary", "arbitrary")),
        name="route",
    )(idx, lstrict, ustrict)


def _dispatch_kernel(meta_ref, dest_ref, h_ref, xs_ref, zero_ref, sem, zsem):
    i = pl.program_id(0)
    tm = h_ref.shape[0]

    def row_copy(t, kk):
        d = dest_ref[t * TOP_K + kk]
        return pltpu.make_async_copy(h_ref.at[pl.ds(t, 1), :], xs_ref.at[pl.ds(d, 1), :], sem)

    def issue(t, carry):
        for kk in range(TOP_K):
            row_copy(t, kk).start(priority=kk % 2)
        return carry

    def drain(t, carry):
        for kk in range(TOP_K):
            row_copy(t, kk).wait()
        return carry

    lax.fori_loop(0, tm, issue, 0)

    @pl.when(i == pl.num_programs(0) - 1)
    def _():
        zero_ref[...] = jnp.zeros_like(zero_ref)

        def pad_copy(r):
            return pltpu.make_async_copy(zero_ref.at[pl.ds(0, 1), :], xs_ref.at[pl.ds(r, 1), :], zsem)

        def blk_copy(b):
            return pltpu.make_async_copy(zero_ref, xs_ref.at[pl.ds(b * MOE_ROWS, MOE_ROWS), :], zsem)

        for e in range(N_EXPERTS):
            lo, hi = meta_ref[0, e], meta_ref[1, e]
            lax.fori_loop(lo, hi, lambda r, c: (pad_copy(r).start(), c)[1], 0)
            lax.fori_loop(lo, hi, lambda r, c: (pad_copy(r).wait(), c)[1], 0)
        n_used = meta_ref[2, N_EXPERTS - 1]
        n_blocks = xs_ref.shape[0] // MOE_ROWS
        lax.fori_loop(n_used, n_blocks, lambda b, c: (blk_copy(b).start(), c)[1], 0)
        lax.fori_loop(n_used, n_blocks, lambda b, c: (blk_copy(b).wait(), c)[1], 0)

    lax.fori_loop(0, tm, drain, 0)


def _dispatch(h2, dest_flat, meta, n_rows):
    n, d = h2.shape
    tm = TM_DMA
    grid_spec = pltpu.PrefetchScalarGridSpec(
        num_scalar_prefetch=1,
        grid=(n // tm,),
        in_specs=[pl.BlockSpec((tm * TOP_K,), lambda i, meta: (i,), memory_space=pltpu.SMEM),
                  pl.BlockSpec((tm, d), lambda i, meta: (i, 0))],
        out_specs=pl.BlockSpec(memory_space=pl.ANY),
        scratch_shapes=[pltpu.VMEM((MOE_ROWS, d), F32), pltpu.SemaphoreType.DMA(()), pltpu.SemaphoreType.DMA(())],
    )
    return pl.pallas_call(
        _dispatch_kernel,
        grid_spec=grid_spec,
        out_shape=jax.ShapeDtypeStruct((n_rows, d), F32),
        compiler_params=_cparams(("arbitrary",)),
        name="dispatch",
    )(meta, dest_flat, h2)


def _experts_kernel(be_ref, xs_ref, wu_ref, bu_ref, wd_ref, bd_ref, y_ref, wu_bf, wd_bf):
    b = pl.program_id(0)
    ff = wd_ref.shape[1]
    prev = be_ref[jnp.maximum(b - 1, 0)]

    @pl.when((b == 0) | (be_ref[b] != prev))
    def _():
        wu_bf[...] = wu_ref[0].astype(BF16)
        wd_bf[...] = wd_ref[0].astype(BF16)

    gu = _dot(xs_ref[...].astype(BF16), wu_bf[...]) + bu_ref[0]
    g = jnp.minimum(gu[:, :ff], SWIGLU_LIMIT)
    u = jnp.clip(gu[:, ff:], -SWIGLU_LIMIT, SWIGLU_LIMIT)
    act = g * jax.nn.sigmoid(SWIGLU_ALPHA * g) * (u + 1.0)
    y_ref[...] = _dot(act.astype(BF16), wd_bf[...]) + bd_ref[0]


def _experts(xs, block_expert, w_up, b_up, w_down, b_down):
    n_rows, d = xs.shape
    e, _, ff2 = w_up.shape
    ff = w_down.shape[1]
    grid_spec = pltpu.PrefetchScalarGridSpec(
        num_scalar_prefetch=1,
        grid=(n_rows // MOE_ROWS,),
        in_specs=[pl.BlockSpec((MOE_ROWS, d), lambda b, be: (b, 0)),
                  pl.BlockSpec((1, d, ff2), lambda b, be: (be[b], 0, 0)),
                  pl.BlockSpec((1, 1, ff2), lambda b, be: (be[b], 0, 0)),
                  pl.BlockSpec((1, ff, d), lambda b, be: (be[b], 0, 0)),
                  pl.BlockSpec((1, 1, d), lambda b, be: (be[b], 0, 0))],
        out_specs=pl.BlockSpec((MOE_ROWS, d), lambda b, be: (b, 0)),
        scratch_shapes=[pltpu.VMEM((d, ff2), BF16), pltpu.VMEM((ff, d), BF16)],
    )
    return pl.pallas_call(
        _experts_kernel,
        grid_spec=grid_spec,
        out_shape=jax.ShapeDtypeStruct((n_rows, d), F32),
        compiler_params=_cparams(("arbitrary",)),
        name="experts",
    )(block_expert, xs, w_up, b_up.reshape(e, 1, ff2), w_down, b_down.reshape(e, 1, d))


def _combine_kernel(dest_ref, x1_ref, gate_ref, p_ref, gple_ref, wpg_ref, wpp_ref, ys_ref, o_ref, rows_ref, sem):
    tm = x1_ref.shape[0]

    def row_copy(t, kk):
        d = dest_ref[t * TOP_K + kk]
        return pltpu.make_async_copy(ys_ref.at[pl.ds(d, 1), :], rows_ref.at[kk, pl.ds(t, 1), :], sem)

    def issue(t, carry):
        for kk in range(TOP_K):
            row_copy(t, kk).start(priority=kk % 2)
        return carry

    def drain(t, carry):
        for kk in range(TOP_K):
            row_copy(t, kk).wait()
        return carry

    lax.fori_loop(0, tm, issue, 0)
    proj = _dot(p_ref[...].astype(BF16), wpp_ref[...])
    lax.fori_loop(0, tm, drain, 0)
    gates = gate_ref[...]
    x2 = x1_ref[...]
    for kk in range(TOP_K):
        x2 = x2 + gates[:, kk:kk + 1] * rows_ref[kk]
    hn = _rms_rows(x2, gple_ref[...]).astype(BF16)
    o_ref[...] = x2 + jax.nn.sigmoid(_dot(hn, wpg_ref[...])) * proj


def _combine(x1, gates, dest_flat, p2d, gple, wpg, wpp, ys):
    n, d = x1.shape
    tm = TM_DMA
    row = lambda c: pl.BlockSpec((tm, c), lambda i: (i, 0))
    return pl.pallas_call(
        _combine_kernel,
        grid=(n // tm,),
        in_specs=[pl.BlockSpec((tm * TOP_K,), lambda i: (i,), memory_space=pltpu.SMEM),
                  row(d), row(LANES), row(p2d.shape[1]), _full(gple.shape), _full(wpg.shape), _full(wpp.shape),
                  pl.BlockSpec(memory_space=pl.ANY)],
        out_specs=row(d),
        out_shape=jax.ShapeDtypeStruct((n, d), F32),
        scratch_shapes=[pltpu.VMEM((TOP_K, tm, d), F32), pltpu.SemaphoreType.DMA(())],
        compiler_params=_cparams(("arbitrary",)),
        name="combine",
    )(dest_flat, x1, gates, p2d, gple, wpg, wpp, ys)


def _rope_tables(seq):
    pos = jnp.arange(seq, dtype=F32)
    inv_freq = ROPE_THETA ** (-jnp.arange(0, MLA_ROPE, 2, dtype=F32) / MLA_ROPE)
    ang = pos[:, None] * inv_freq[None, :]
    c, s = jnp.cos(ang), jnp.sin(ang)
    reps = LANES // MLA_ROPE
    return jnp.tile(jnp.concatenate([c, c], -1), (1, reps)), jnp.tile(jnp.concatenate([-s, s], -1), (1, reps))


def _layer(x2d, p2d, batch, seq, cos, sin, attn_norm, w_in, q_a_norm, w_uq, kv_a_norm, w_ukv,
           mla_q_nope_norm, mla_q_rope_norm, mla_k_nope_norm, mla_k_rope_norm,
           fox_q_norm, fox_k_norm, fox_f_bias, w_o_mla, w_o_fox, w_o,
           ffn_norm, w_router, b_router, w_up, b_up, w_down, b_down, ple_norm, w_ple_gate, w_ple_proj):
    n, d = x2d.shape
    nf = FOX_HEADS * FOX_HD
    row1 = lambda v: v.reshape(1, -1).astype(F32)

    o = np.cumsum([0, MLA_Q_LORA, MLA_KV_LORA, MLA_ROPE, nf, nf, nf, FOX_HEADS, d, d])
    wa = jnp.concatenate([w_in[:, o[0]:o[3]], w_in[:, o[6]:o[7]],
                          jnp.zeros((d, ZA_COLS - int(o[3]) - FOX_HEADS), w_in.dtype)], axis=1).astype(BF16)
    wf = w_in[:, o[3]:o[6]].astype(BF16)
    wg = w_in[:, o[7]:o[9]].astype(BF16)
    za, zf, zg = _inproj(x2d, row1(attn_norm), wa, wf, wg)

    wq3 = w_uq.reshape(MLA_Q_LORA, MLA_HEADS, MLA_NOPE + MLA_ROPE)
    wuq = jnp.concatenate([wq3[:, :, :MLA_NOPE].reshape(MLA_Q_LORA, -1),
                           wq3[:, :, MLA_NOPE:].reshape(MLA_Q_LORA, -1)], axis=1).astype(BF16)
    wkv3 = w_ukv.reshape(MLA_KV_LORA, MLA_HEADS, MLA_NOPE + MLA_V)
    wukv = jnp.concatenate([wkv3[:, :, :MLA_NOPE].reshape(MLA_KV_LORA, -1),
                            wkv3[:, :, MLA_NOPE:].reshape(MLA_KV_LORA, -1)], axis=1).astype(BF16)
    mla_scale = (MLA_NOPE + MLA_ROPE) ** -0.5 * LOG2E
    fox_scale = FOX_HD ** -0.5 * LOG2E
    gkr = jnp.concatenate([mla_k_rope_norm.astype(F32), jnp.zeros((LANES - MLA_ROPE,), F32)])
    fb = jnp.zeros((LANES,), F32).at[FLOGIT_LANE:FLOGIT_LANE + FOX_HEADS].set(fox_f_bias.astype(F32))
    gains = [row1(q_a_norm), row1(kv_a_norm),
             row1(jnp.tile(mla_q_nope_norm, MLA_HEADS) * mla_scale),
             row1(jnp.tile(mla_q_rope_norm, MLA_HEADS) * mla_scale),
             row1(jnp.tile(mla_k_nope_norm, MLA_HEADS)), row1(gkr),
             row1(jnp.tile(fox_q_norm, FOX_HEADS) * fox_scale), row1(jnp.tile(fox_k_norm, FOX_HEADS)), row1(fb)]
    qm, km, vm, qf, kf, vf = _prep(za, zf, cos, sin, wuq, wukv, gains, batch, seq)

    o_mla = _attention(qm, km, vm, hpb=4, dv=MLA_V).reshape(n, MLA_HEADS * MLA_V)
    o_fox = _attention(qf, kf, vf, hpb=4, dv=FOX_HD).reshape(n, nf)

    wr = jnp.concatenate([w_router.astype(F32), jnp.zeros((d, LANES - N_EXPERTS), F32)], axis=1)
    br = jnp.concatenate([b_router.astype(F32), jnp.zeros((LANES - N_EXPERTS,), F32)]).reshape(1, LANES)
    x1, h2, idx, gates = _outproj(x2d, o_mla, o_fox, zg, w_o_mla.astype(BF16), w_o_fox.astype(BF16),
                                  w_o.astype(BF16), row1(ffn_norm), wr, br)

    n_blocks = n * TOP_K // MOE_ROWS + N_EXPERTS
    dest, be, meta = _route(idx, n_blocks)
    dest_flat = dest[:, :TOP_K].reshape(-1)
    xs = _dispatch(h2, dest_flat, meta, n_blocks * MOE_ROWS)
    ys = _experts(xs, be[:n_blocks, 0], w_up, b_up, w_down, b_down)
    return _combine(x1, gates, dest_flat, p2d, row1(ple_norm), w_ple_gate.astype(BF16),
                    w_ple_proj.astype(BF16), ys)


def kernel(x, p, attn_norm, w_in, q_a_norm, w_uq, kv_a_norm, w_ukv, mla_q_nope_norm, mla_q_rope_norm,
           mla_k_nope_norm, mla_k_rope_norm, fox_q_norm, fox_k_norm, fox_f_bias, w_o_mla, w_o_fox, w_o,
           ffn_norm, w_router, b_router, w_up, b_up, w_down, b_down, ple_norm, w_ple_gate, w_ple_proj):
    batch, seq, d = x.shape
    depth = p.shape[0]
    cos, sin = _rope_tables(seq)
    x2d = x.reshape(batch * seq, d)
    for i in range(depth):
        x2d = _layer(x2d, p[i].reshape(batch * seq, -1), batch, seq, cos, sin,
                     attn_norm[i], w_in[i], q_a_norm[i], w_uq[i], kv_a_norm[i], w_ukv[i],
                     mla_q_nope_norm[i], mla_q_rope_norm[i], mla_k_nope_norm[i], mla_k_rope_norm[i],
                     fox_q_norm[i], fox_k_norm[i], fox_f_bias[i], w_o_mla[i], w_o_fox[i], w_o[i],
                     ffn_norm[i], w_router[i], b_router[i], w_up[i], b_up[i], w_down[i], b_down[i],
                     ple_norm[i], w_ple_gate[i], w_ple_proj[i])
    return x2d.reshape(batch, seq, d)
```

```python
import functools

import numpy as np
import jax
import jax.numpy as jnp
from jax import lax
from jax.experimental import pallas as pl
from jax.experimental.pallas import tpu as pltpu

F32 = jnp.float32
BF16 = jnp.bfloat16
I32 = jnp.int32

EPS = 1e-6
ROPE_THETA = 10000.0
MLA_HEADS = 8
MLA_NOPE = 128
MLA_ROPE = 64
MLA_V = 128
MLA_Q_LORA = 256
MLA_KV_LORA = 256
FOX_HEADS = 16
FOX_HD = 64
N_EXPERTS = 32
TOP_K = 4
SWIGLU_LIMIT = 7.0
SWIGLU_ALPHA = 1.702

LANES = 128
ZA_COLS = 640
FLOGIT_LANE = 64
FOX_DK = 128
V_PAD = 16
MOE_ROWS = 256
NEG_BIG = -1e30
LOG2E = 1.4426950408889634
VMEM_LIMIT = 56 * 1024 * 1024

TM_PROJ = 512
TM_DMA = 256
ATTN_TQ = 512


def _cparams(sem):
    return pltpu.CompilerParams(dimension_semantics=sem, vmem_limit_bytes=VMEM_LIMIT)


def _full(shape):
    nd = len(shape)
    return pl.BlockSpec(shape, lambda *_: (0,) * nd)


def _rms_rows(x, gain):
    return x * lax.rsqrt(jnp.mean(x * x, axis=-1, keepdims=True) + EPS) * gain


def _split3(x):
    hi = x.astype(BF16)
    r1 = x - hi.astype(F32)
    mid = r1.astype(BF16)
    lo = (r1 - mid.astype(F32)).astype(BF16)
    return hi, mid, lo


def _dot(a, b):
    return jnp.dot(a, b, preferred_element_type=F32)


def _store_token_tiles(ref, x):
    nch = x.shape[1] // LANES
    for c in range(nch):
        ref[pl.ds(c, x.shape[0], stride=nch), :] = x[:, c * LANES:(c + 1) * LANES]


def _load_token_tiles(ref, rows, nch):
    return jnp.concatenate([ref[pl.ds(c, rows, stride=nch), :] for c in range(nch)], axis=-1)


def _inproj_kernel(x_ref, g_ref, wa_ref, wf_ref, wg_ref, za_ref, zf_ref, zg_ref):
    hb = _rms_rows(x_ref[...], g_ref[...]).astype(BF16)
    za_ref[...] = _dot(hb, wa_ref[...])
    zf_ref[...] = _dot(hb, wf_ref[...]).astype(BF16)
    zg_ref[...] = jax.nn.sigmoid(_dot(hb, wg_ref[...])).astype(BF16)


def _inproj(x2d, gain, wa, wf, wg):
    n, d = x2d.shape
    tm = TM_PROJ
    row = lambda c: pl.BlockSpec((tm, c), lambda i: (i, 0))
    return pl.pallas_call(
        _inproj_kernel,
        grid=(n // tm,),
        in_specs=[row(d), _full(gain.shape), _full(wa.shape), _full(wf.shape), _full(wg.shape)],
        out_specs=[row(wa.shape[1]), row(wf.shape[1]), row(wg.shape[1])],
        out_shape=[jax.ShapeDtypeStruct((n, wa.shape[1]), F32),
                   jax.ShapeDtypeStruct((n, wf.shape[1]), BF16),
                   jax.ShapeDtypeStruct((n, wg.shape[1]), BF16)],
        compiler_params=_cparams(("arbitrary",)),
        name="inproj",
    )(x2d, gain, wa, wf, wg)


def _group_rms(x, gmat, gmat_t, group, gain):
    ss = _dot((x * x).astype(BF16), gmat)
    inv = lax.rsqrt(ss * (1.0 / group) + EPS)
    inv_hi = inv.astype(BF16)
    inv_lo = (inv - inv_hi.astype(F32)).astype(BF16)
    inv_full = _dot(inv_hi, gmat_t) + _dot(inv_lo, gmat_t)
    return x * inv_full * gain


def _rot_half(x):
    c = x.shape[-1]
    lane = lax.broadcasted_iota(I32, x.shape, 1)
    first = (lane & 63) < 32
    return jnp.where(first, pltpu.roll(x, c - 32, 1), pltpu.roll(x, 32, 1))


def _log_sigmoid(x):
    return jnp.minimum(x, 0.0) - jnp.log1p(jnp.exp(-jnp.abs(x)))


def _prep_kernel(za_ref, zf_ref, cos_ref, sin_ref, wuq_ref, wukv_ref,
                 gqa_ref, gkva_ref, gqn_ref, gqr_ref, gkn_ref, gkr_ref, gfq_ref, gfk_ref, fb_ref,
                 g128_ref, g128t_ref, g64_ref, g64t_ref, tri_ref, eq_ref, ek_ref, oneq_ref, onek_ref,
                 qm_ref, km_ref, vm_ref, qf_ref, kf_ref, vf_ref, carry_ref):
    si = pl.program_id(1)
    za = za_ref[...]
    cos = cos_ref[...]
    sin = sin_ref[...]
    g128, g128t = g128_ref[...], g128t_ref[...]
    g64, g64t = g64_ref[...], g64t_ref[...]

    cq = _rms_rows(za[:, 0:MLA_Q_LORA], gqa_ref[...]).astype(BF16)
    ckv = _rms_rows(za[:, MLA_Q_LORA:MLA_Q_LORA + MLA_KV_LORA], gkva_ref[...]).astype(BF16)
    last = za[:, ZA_COLS - LANES:ZA_COLS]
    q = _dot(cq, wuq_ref[...])
    kv = _dot(ckv, wukv_ref[...])
    n_nope = MLA_HEADS * MLA_NOPE
    n_rope = MLA_HEADS * MLA_ROPE
    qn = _group_rms(q[:, :n_nope], g128, g128t, MLA_NOPE, gqn_ref[...])
    qr = _group_rms(q[:, n_nope:], g64[:n_rope], g64t[:, :n_rope], MLA_ROPE, gqr_ref[...])
    cos4 = jnp.concatenate([cos] * (n_rope // LANES), axis=-1)
    sin4 = jnp.concatenate([sin] * (n_rope // LANES), axis=-1)
    qr = qr * cos4 + _rot_half(qr) * sin4
    kn = _group_rms(kv[:, :n_nope], g128, g128t, MLA_NOPE, gkn_ref[...])
    lane = lax.broadcasted_iota(I32, last.shape, 1)
    kr_ss = jnp.sum(jnp.where(lane < MLA_ROPE, last * last, 0.0), axis=-1, keepdims=True)
    kr = last * lax.rsqrt(kr_ss * (1.0 / MLA_ROPE) + EPS) * gkr_ref[...]
    kr = kr * cos + _rot_half(kr) * sin
    kr_b = kr[:, :MLA_ROPE].astype(BF16)
    tm = za.shape[0]
    ones_rows = jnp.where(lax.broadcasted_iota(I32, (V_PAD, tm), 0) == 0, 1.0, 0.0).astype(BF16)
    qn_t = qn.T.astype(BF16)
    qr_t = qr.T.astype(BF16)
    v_t = kv[:, n_nope:].T.astype(BF16)
    for h in range(MLA_HEADS):
        qm_ref[0, h, 0:MLA_NOPE, :] = qn_t[h * MLA_NOPE:(h + 1) * MLA_NOPE, :]
        qm_ref[0, h, MLA_NOPE:MLA_NOPE + MLA_ROPE, :] = qr_t[h * MLA_ROPE:(h + 1) * MLA_ROPE, :]
        km_ref[0, h, :, 0:MLA_NOPE] = kn[:, h * MLA_NOPE:(h + 1) * MLA_NOPE].astype(BF16)
        km_ref[0, h, :, MLA_NOPE:MLA_NOPE + MLA_ROPE] = kr_b
        vm_ref[0, h, 0:MLA_V, :] = v_t[h * MLA_V:(h + 1) * MLA_V, :]
        vm_ref[0, h, MLA_V:MLA_V + V_PAD, :] = ones_rows

    nf = FOX_HEADS * FOX_HD
    zf = zf_ref[...]
    fq = _group_rms(zf[:, 0:nf].astype(F32), g64, g64t, FOX_HD, gfq_ref[...])
    fk = _group_rms(zf[:, nf:2 * nf].astype(F32), g64, g64t, FOX_HD, gfk_ref[...])
    fv = zf[:, 2 * nf:3 * nf]

    @pl.when(si == 0)
    def _():
        carry_ref[...] = jnp.zeros_like(carry_ref)

    logf = _log_sigmoid(last + fb_ref[...])
    l_hi, l_mid, l_lo = _split3(logf)
    tri = tri_ref[...]
    cum = carry_ref[...] + (_dot(tri, l_hi) + _dot(tri, l_mid) + _dot(tri, l_lo))
    tm = cum.shape[0]
    carry_ref[...] = cum[tm - 1:tm, :]
    fcat = jnp.concatenate(_split3(cum * LOG2E), axis=-1)
    augq = _dot(fcat, eq_ref[...]) + oneq_ref[...]
    augk = _dot(fcat, ek_ref[...]) + onek_ref[...]
    fq_t = fq.T.astype(BF16)
    augq_t = augq.T.astype(BF16)
    fv_t = fv.astype(F32).T.astype(BF16)
    for h in range(FOX_HEADS):
        sl = slice(h * FOX_HD, (h + 1) * FOX_HD)
        qf_ref[0, h, 0:FOX_HD, :] = fq_t[sl, :]
        qf_ref[0, h, FOX_HD:2 * FOX_HD, :] = augq_t[sl, :]
        kf_ref[0, h] = jnp.concatenate([fk[:, sl], augk[:, sl]], axis=-1).astype(BF16)
        vf_ref[0, h, 0:FOX_HD, :] = fv_t[sl, :]
        vf_ref[0, h, FOX_HD:FOX_HD + V_PAD, :] = ones_rows


def _group_matrix(cols, group):
    m = np.zeros((cols, LANES), np.float32)
    m[np.arange(cols), np.arange(cols) // group] = 1.0
    return m


def _bias_column_matrices():
    nf = FOX_HEADS * FOX_HD
    eq = np.zeros((3 * LANES, nf), np.float32)
    ek = np.zeros((3 * LANES, nf), np.float32)
    oneq = np.zeros((1, nf), np.float32)
    onek = np.zeros((1, nf), np.float32)
    for h in range(FOX_HEADS):
        for t in range(3):
            eq[t * LANES + FLOGIT_LANE + h, h * FOX_HD + t] = 1.0
            ek[t * LANES + FLOGIT_LANE + h, h * FOX_HD + 3 + t] = -1.0
            oneq[0, h * FOX_HD + 3 + t] = 1.0
            onek[0, h * FOX_HD + t] = 1.0
    return eq, ek, oneq, onek


def _prep(za, zf, cos, sin, wuq, wukv, gains, batch, seq):
    tm = TM_PROJ
    nt = seq // tm
    g128 = _group_matrix(MLA_HEADS * MLA_NOPE, MLA_NOPE)
    g64 = _group_matrix(FOX_HEADS * FOX_HD, FOX_HD)
    tri = np.tril(np.ones((tm, tm), np.float32))
    eq, ek, oneq, onek = _bias_column_matrices()
    consts = [jnp.asarray(g128, BF16), jnp.asarray(g128.T, BF16), jnp.asarray(g64, BF16), jnp.asarray(g64.T, BF16),
              jnp.asarray(tri, BF16), jnp.asarray(eq, BF16), jnp.asarray(ek, BF16),
              jnp.asarray(oneq, F32), jnp.asarray(onek, F32)]
    row = lambda c: pl.BlockSpec((tm, c), lambda b, s: (b * nt + s, 0))
    tab = pl.BlockSpec((tm, LANES), lambda b, s: (s, 0))
    head = lambda h, d: pl.BlockSpec((1, h, tm, d), lambda b, s: (b, 0, s, 0))
    head_t = lambda h, d: pl.BlockSpec((1, h, d, tm), lambda b, s: (b, 0, 0, s))
    hshape = lambda h, d: jax.ShapeDtypeStruct((batch, h, seq, d), BF16)
    hshape_t = lambda h, d: jax.ShapeDtypeStruct((batch, h, d, seq), BF16)
    dk = MLA_NOPE + MLA_ROPE
    return pl.pallas_call(
        _prep_kernel,
        grid=(batch, nt),
        in_specs=[row(za.shape[1]), row(zf.shape[1]), tab, tab, _full(wuq.shape), _full(wukv.shape)]
                 + [_full(g.shape) for g in gains] + [_full(c.shape) for c in consts],
        out_specs=[head_t(MLA_HEADS, dk), head(MLA_HEADS, dk), head_t(MLA_HEADS, MLA_V + V_PAD),
                   head_t(FOX_HEADS, FOX_DK), head(FOX_HEADS, FOX_DK), head_t(FOX_HEADS, FOX_HD + V_PAD)],
        out_shape=[hshape_t(MLA_HEADS, dk), hshape(MLA_HEADS, dk), hshape_t(MLA_HEADS, MLA_V + V_PAD),
                   hshape_t(FOX_HEADS, FOX_DK), hshape(FOX_HEADS, FOX_DK), hshape_t(FOX_HEADS, FOX_HD + V_PAD)],
        scratch_shapes=[pltpu.VMEM((1, LANES), F32)],
        compiler_params=_cparams(("arbitrary", "arbitrary")),
        name="prep",
    )(za, zf, cos, sin, wuq, wukv, *gains, *consts)


def _attn_kernel(qt_ref, k_ref, vt_ref, o_ref, *scratch, tq, tk, hpb, dv):
    qi = pl.program_id(2)
    m_refs, acc_refs, s_refs = scratch[:hpb], scratch[hpb:2 * hpb], scratch[2 * hpb:]
    n_ahead = len(s_refs)

    def score(hh, j):
        return _dot(k_ref[0, hh, pl.ds(pl.multiple_of(j * tk, tk), tk), :], qt_ref[0, hh])

    def kv_block(j, diagonal):
        start = pl.multiple_of(j * tk, tk)
        state = [(m_refs[hh][...], acc_refs[hh][...]) for hh in range(hpb)]
        ready = {hh: s_refs[hh][...] for hh in range(n_ahead)}
        upcoming = list(range(n_ahead, hpb)) + ([] if diagonal else [("next", hh) for hh in range(n_ahead)])

        def issue_next_score():
            if not upcoming:
                return
            item = upcoming.pop(0)
            if isinstance(item, tuple):
                s_refs[item[1]][...] = score(item[1], j + 1)
            else:
                ready[item] = score(item, j)

        issue_next_score()
        results = []
        for hh in range(hpb):
            s, (m_old, acc_old) = ready.pop(hh), state[hh]
            if diagonal:
                key = lax.broadcasted_iota(I32, s.shape, 0)
                qry = lax.broadcasted_iota(I32, s.shape, 1)
                s = jnp.where(key <= qry, s, NEG_BIG)
            m_new = jnp.maximum(m_old, jnp.max(s, axis=0, keepdims=True))
            p = jnp.exp2((s - m_new).astype(BF16))
            pv = _dot(vt_ref[0, hh, :, pl.ds(start, tk)], p)
            results.append((m_new, acc_old * jnp.exp2(m_old - m_new) + pv))
            issue_next_score()
        while upcoming:
            issue_next_score()
        for hh in range(hpb):
            m_refs[hh][...] = results[hh][0]
            acc_refs[hh][...] = results[hh][1]

    for hh in range(hpb):
        m_refs[hh][...] = jnp.full(m_refs[hh].shape, NEG_BIG, F32)
        acc_refs[hh][...] = jnp.zeros(acc_refs[hh].shape, F32)
    for hh in range(n_ahead):
        s_refs[hh][...] = score(hh, 0)

    def full_block(j, carry):
        kv_block(j, False)
        return carry

    lax.fori_loop(0, qi, full_block, 0)
    kv_block(qi, True)
    outs = []
    for hh in range(hpb):
        acc = acc_refs[hh][...]
        outs.append(acc[:dv, :] / acc[dv:dv + 1, :])
    o_ref[0] = jnp.concatenate(outs, axis=0).T.astype(o_ref.dtype)


def _attention(qt, k, vt, hpb, dv):
    b, h, s, dk = k.shape
    dvp = vt.shape[2]
    tq = tk = ATTN_TQ
    assert s % tq == 0 and h % hpb == 0 and (hpb * dv) % LANES == 0 and dvp == dv + V_PAD
    kern = functools.partial(_attn_kernel, tq=tq, tk=tk, hpb=hpb, dv=dv)
    kv_bytes = hpb * s * (-(-dk // LANES) * LANES + dvp) * 2
    kv_mode = {} if 2 * kv_bytes <= VMEM_LIMIT // 2 else {"pipeline_mode": pl.Buffered(1)}
    return pl.pallas_call(
        kern,
        grid=(b, h // hpb, s // tq),
        in_specs=[pl.BlockSpec((1, hpb, dk, tq), lambda bi, hi, qi: (bi, hi, 0, qi)),
                  pl.BlockSpec((1, hpb, s, dk), lambda bi, hi, qi: (bi, hi, 0, 0), **kv_mode),
                  pl.BlockSpec((1, hpb, dvp, s), lambda bi, hi, qi: (bi, hi, 0, 0), **kv_mode)],
        out_specs=pl.BlockSpec((1, tq, hpb * dv), lambda bi, hi, qi: (bi, qi, hi)),
        out_shape=jax.ShapeDtypeStruct((b, s, h * dv), BF16),
        scratch_shapes=([pltpu.VMEM((1, tq), F32)] * hpb + [pltpu.VMEM((dvp, tq), F32)] * hpb
                        + [pltpu.VMEM((tk, tq), F32)] * min(2, hpb)),
        compiler_params=_cparams(("arbitrary", "arbitrary", "arbitrary")),
        name="attention",
    )(qt, k, vt)


def _outproj_kernel(x_ref, om_ref, of_ref, zg_ref, wom_ref, wof_ref, wo_ref, gffn_ref, wr_ref, br_ref,
                    x1_ref, h2_ref, idx_ref, gate_ref):
    d = x_ref.shape[1]
    zg = zg_ref[...].astype(F32)
    mixed = zg[:, :d] * _dot(om_ref[...], wom_ref[...]) + zg[:, d:] * _dot(of_ref[...], wof_ref[...])
    x1 = x_ref[...] + _dot(mixed.astype(BF16), wo_ref[...])
    x1_ref[...] = x1
    h2 = _rms_rows(x1, gffn_ref[...])
    _store_token_tiles(h2_ref, h2)
    logits = jnp.dot(h2, wr_ref[...], preferred_element_type=F32, precision=lax.Precision.HIGHEST) + br_ref[...]
    lane = lax.broadcasted_iota(I32, logits.shape, 1)
    lane_f = lane.astype(F32)
    work = jnp.where(lane < N_EXPERTS, logits, -jnp.inf)
    vals, idxs = [], []
    for _ in range(TOP_K):
        m = jnp.max(work, axis=-1, keepdims=True)
        idx = jnp.min(jnp.where(work == m, lane_f, float(LANES)), axis=-1, keepdims=True)
        vals.append(m)
        idxs.append(idx)
        work = jnp.where(lane_f == idx, -jnp.inf, work)
    exps = [jnp.exp(v - vals[0]) for v in vals]
    denom = exps[0] + exps[1] + exps[2] + exps[3]
    idx_out = jnp.zeros(logits.shape, F32)
    gate_out = jnp.zeros(logits.shape, F32)
    for kk in range(TOP_K):
        idx_out = jnp.where(lane == kk, idxs[kk], idx_out)
        gate_out = jnp.where(lane == kk, exps[kk] / denom, gate_out)
    idx_ref[...] = idx_out.astype(I32)
    gate_ref[...] = gate_out


def _outproj(x2d, o_mla, o_fox, zg, wom, wof, wo, gffn, wr, br):
    n, d = x2d.shape
    tm = TM_PROJ
    row = lambda c: pl.BlockSpec((tm, c), lambda i: (i, 0))
    return pl.pallas_call(
        _outproj_kernel,
        grid=(n // tm,),
        in_specs=[row(d), row(o_mla.shape[1]), row(o_fox.shape[1]), row(zg.shape[1]),
                  _full(wom.shape), _full(wof.shape), _full(wo.shape), _full(gffn.shape),
                  _full(wr.shape), _full(br.shape)],
        out_specs=[row(d), pl.BlockSpec((tm * (d // LANES), LANES), lambda i: (i, 0)), row(LANES), row(LANES)],
        out_shape=[jax.ShapeDtypeStruct((n, d), F32), jax.ShapeDtypeStruct((n * (d // LANES), LANES), F32),
                   jax.ShapeDtypeStruct((n, LANES), I32), jax.ShapeDtypeStruct((n, LANES), F32)],
        compiler_params=_cparams(("arbitrary",)),
        name="outproj",
    )(x2d, o_mla, o_fox, zg, wom, wof, wo, gffn, wr, br)


def _route_kernel(idx_ref, lstrict_ref, ustrict_ref, dest_ref, be_ref, meta_ref, cnt_ref, start_ref, run_ref):
    phase = pl.program_id(0)
    i = pl.program_id(1)
    idx = idx_ref[...]
    lane = lax.broadcasted_iota(I32, idx.shape, 1)
    hit = [lane == idx[:, kk:kk + 1] for kk in range(TOP_K)]
    onehot = jnp.zeros(idx.shape, F32)
    for hk in hit:
        onehot = jnp.where(hk, 1.0, onehot)
    tile_cnt = jnp.sum(onehot, axis=0, keepdims=True)

    @pl.when((phase == 0) & (i == 0))
    def _():
        cnt_ref[...] = jnp.zeros_like(cnt_ref)

    @pl.when(phase == 0)
    def _():
        cnt_ref[...] += tile_cnt

    @pl.when((phase == 1) & (i == 0))
    def _():
        cnt = cnt_ref[...]
        nblk = jnp.floor((cnt + (MOE_ROWS - 1.0)) * (1.0 / MOE_ROWS))
        nblk8 = jnp.broadcast_to(nblk, (8, LANES)).astype(BF16)
        start = _dot(nblk8, ustrict_ref[...])[0:1, :]
        start_ref[...] = start
        run_ref[...] = jnp.zeros_like(run_ref)
        end = start + nblk
        nb = be_ref.shape[0]
        bid = lax.broadcasted_iota(I32, (nb, LANES), 0).astype(F32)
        lane_b = lax.broadcasted_iota(I32, (nb, LANES), 1)
        owned = jnp.where((lane_b < N_EXPERTS) & (end <= bid), 1.0, 0.0)
        be = jnp.minimum(jnp.sum(owned, axis=-1, keepdims=True), N_EXPERTS - 1.0)
        be_ref[...] = jnp.broadcast_to(be, (nb, LANES)).astype(I32)
        pad_lo = start * MOE_ROWS + cnt
        pad_hi = end * MOE_ROWS
        row8 = lax.broadcasted_iota(I32, (8, LANES), 0)
        meta = jnp.where(row8 == 0, pad_lo, jnp.where(row8 == 1, pad_hi, jnp.where(row8 == 2, end, 0.0)))
        meta_ref[...] = meta.astype(I32)

    @pl.when(phase == 1)
    def _():
        rank = _dot(lstrict_ref[...], onehot.astype(BF16))
        slot = start_ref[...] * MOE_ROWS + run_ref[...] + rank
        out = jnp.zeros(idx.shape, F32)
        for kk in range(TOP_K):
            dk = jnp.sum(jnp.where(hit[kk], slot, 0.0), axis=-1, keepdims=True)
            out = jnp.where(lane == kk, dk, out)
        dest_ref[...] = out.astype(I32)
        run_ref[...] += tile_cnt


def _route(idx, n_blocks):
    n = idx.shape[0]
    tm = TM_PROJ
    nt = n // tm
    nb = (n_blocks + 7) // 8 * 8
    lstrict = jnp.asarray(np.tril(np.ones((tm, tm), np.float32), -1), BF16)
    ustrict = jnp.asarray(np.triu(np.ones((LANES, LANES), np.float32), 1), BF16)
    return pl.pallas_call(
        _route_kernel,
        grid=(2, nt),
        in_specs=[pl.BlockSpec((tm, LANES), lambda p, i: (i, 0)), _full(lstrict.shape), _full(ustrict.shape)],
        out_specs=[pl.BlockSpec((tm, LANES), lambda p, i: (i * p, 0)), _full((nb, LANES)), _full((8, LANES))],
        out_shape=[jax.ShapeDtypeStruct((n, LANES), I32), jax.ShapeDtypeStruct((nb, LANES), I32),
                   jax.ShapeDtypeStruct((8, LANES), I32)],
        scratch_shapes=[pltpu.VMEM((1, LANES), F32), pltpu.VMEM((1, LANES), F32), pltpu.VMEM((1, LANES), F32)],
        compiler_params=_cparams(("arbitrary", "arbitrary")),
        name="route",
    )(idx, lstrict, ustrict)


def _dispatch_kernel(meta_ref, dest_ref, h_ref, xs_ref, zero_ref, sem, zsem, *, nch):
    i = pl.program_id(0)
    tm = h_ref.shape[0] // nch

    def tile(ref, r):
        return ref.at[pl.ds(pl.multiple_of(r * nch, nch), nch), :]

    def row_copy(t, kk):
        return pltpu.make_async_copy(tile(h_ref, t), tile(xs_ref, dest_ref[t * TOP_K + kk]), sem)

    def issue(t, carry):
        for kk in range(TOP_K):
            row_copy(t, kk).start(priority=kk % 2)
        return carry

    def drain(t, carry):
        for kk in range(TOP_K):
            row_copy(t, kk).wait()
        return carry

    lax.fori_loop(0, tm, issue, 0)

    @pl.when(i == pl.num_programs(0) - 1)
    def _():
        zero_ref[...] = jnp.zeros_like(zero_ref)
        blk = MOE_ROWS * nch

        def pad_copy(r):
            return pltpu.make_async_copy(tile(zero_ref, 0), tile(xs_ref, r), zsem)

        def blk_copy(b):
            return pltpu.make_async_copy(zero_ref, xs_ref.at[pl.ds(pl.multiple_of(b * blk, blk), blk), :], zsem)

        for e in range(N_EXPERTS):
            lo, hi = meta_ref[0, e], meta_ref[1, e]
            lax.fori_loop(lo, hi, lambda r, c: (pad_copy(r).start(), c)[1], 0)
            lax.fori_loop(lo, hi, lambda r, c: (pad_copy(r).wait(), c)[1], 0)
        n_used = meta_ref[2, N_EXPERTS - 1]
        n_blocks = xs_ref.shape[0] // blk
        lax.fori_loop(n_used, n_blocks, lambda b, c: (blk_copy(b).start(), c)[1], 0)
        lax.fori_loop(n_used, n_blocks, lambda b, c: (blk_copy(b).wait(), c)[1], 0)

    lax.fori_loop(0, tm, drain, 0)


def _dispatch(h2t, dest_flat, meta, n_rows, nch):
    n = h2t.shape[0] // nch
    tm = TM_DMA
    grid_spec = pltpu.PrefetchScalarGridSpec(
        num_scalar_prefetch=1,
        grid=(n // tm,),
        in_specs=[pl.BlockSpec((tm * TOP_K,), lambda i, meta: (i,), memory_space=pltpu.SMEM),
                  pl.BlockSpec((tm * nch, LANES), lambda i, meta: (i, 0))],
        out_specs=pl.BlockSpec(memory_space=pl.ANY),
        scratch_shapes=[pltpu.VMEM((MOE_ROWS * nch, LANES), F32), pltpu.SemaphoreType.DMA(()),
                        pltpu.SemaphoreType.DMA(())],
    )
    return pl.pallas_call(
        functools.partial(_dispatch_kernel, nch=nch),
        grid_spec=grid_spec,
        out_shape=jax.ShapeDtypeStruct((n_rows * nch, LANES), F32),
        compiler_params=_cparams(("arbitrary",)),
        name="dispatch",
    )(meta, dest_flat, h2t)


def _experts_kernel(be_ref, xs_ref, wu_ref, bu_ref, wd_ref, bd_ref, y_ref, wu_bf, wd_bf):
    b = pl.program_id(0)
    ff = wd_ref.shape[1]
    prev = be_ref[jnp.maximum(b - 1, 0)]

    @pl.when((b == 0) | (be_ref[b] != prev))
    def _():
        wu_bf[...] = wu_ref[0].astype(BF16)
        wd_bf[...] = wd_ref[0].astype(BF16)

    nch = wu_ref.shape[1] // LANES
    x = _load_token_tiles(xs_ref, xs_ref.shape[0] // nch, nch)
    gu = _dot(x.astype(BF16), wu_bf[...]) + bu_ref[0]
    g = jnp.minimum(gu[:, :ff], SWIGLU_LIMIT)
    u = jnp.clip(gu[:, ff:], -SWIGLU_LIMIT, SWIGLU_LIMIT)
    act = g * jax.nn.sigmoid(SWIGLU_ALPHA * g) * (u + 1.0)
    _store_token_tiles(y_ref, _dot(act.astype(BF16), wd_bf[...]) + bd_ref[0])


def _experts(xs, block_expert, w_up, b_up, w_down, b_down):
    e, d, ff2 = w_up.shape
    ff = w_down.shape[1]
    nch = d // LANES
    n_rows = xs.shape[0] // nch
    grid_spec = pltpu.PrefetchScalarGridSpec(
        num_scalar_prefetch=1,
        grid=(n_rows // MOE_ROWS,),
        in_specs=[pl.BlockSpec((MOE_ROWS * nch, LANES), lambda b, be: (b, 0)),
                  pl.BlockSpec((1, d, ff2), lambda b, be: (be[b], 0, 0)),
                  pl.BlockSpec((1, 1, ff2), lambda b, be: (be[b], 0, 0)),
                  pl.BlockSpec((1, ff, d), lambda b, be: (be[b], 0, 0)),
                  pl.BlockSpec((1, 1, d), lambda b, be: (be[b], 0, 0))],
        out_specs=pl.BlockSpec((MOE_ROWS * nch, LANES), lambda b, be: (b, 0)),
        scratch_shapes=[pltpu.VMEM((d, ff2), BF16), pltpu.VMEM((ff, d), BF16)],
    )
    return pl.pallas_call(
        _experts_kernel,
        grid_spec=grid_spec,
        out_shape=jax.ShapeDtypeStruct((n_rows * nch, LANES), F32),
        compiler_params=_cparams(("arbitrary",)),
        name="experts",
    )(block_expert, xs, w_up, b_up.reshape(e, 1, ff2), w_down, b_down.reshape(e, 1, d))


def _combine_kernel(dest_ref, x1_ref, gate_ref, p_ref, gple_ref, wpg_ref, wpp_ref, ys_ref, o_ref, rows_ref, sem):
    tm, d = x1_ref.shape
    nch = d // LANES

    def row_copy(t, kk):
        src = pl.multiple_of(dest_ref[t * TOP_K + kk] * nch, nch)
        dst = pl.multiple_of(t * nch, nch)
        return pltpu.make_async_copy(ys_ref.at[pl.ds(src, nch), :], rows_ref.at[kk, pl.ds(dst, nch), :], sem)

    def issue(t, carry):
        for kk in range(TOP_K):
            row_copy(t, kk).start(priority=kk % 2)
        return carry

    def drain(t, carry):
        for kk in range(TOP_K):
            row_copy(t, kk).wait()
        return carry

    lax.fori_loop(0, tm, issue, 0)
    proj = _dot(p_ref[...].astype(BF16), wpp_ref[...])
    lax.fori_loop(0, tm, drain, 0)
    gates = gate_ref[...]
    moe = gates[:, 0:1] * _load_token_tiles(rows_ref.at[0], tm, nch)
    for kk in range(1, TOP_K):
        moe = moe + gates[:, kk:kk + 1] * _load_token_tiles(rows_ref.at[kk], tm, nch)
    x2 = x1_ref[...] + moe
    hn = _rms_rows(x2, gple_ref[...]).astype(BF16)
    o_ref[...] = x2 + jax.nn.sigmoid(_dot(hn, wpg_ref[...])) * proj


def _combine(x1, gates, dest_flat, p2d, gple, wpg, wpp, ys):
    n, d = x1.shape
    tm = TM_DMA
    row = lambda c: pl.BlockSpec((tm, c), lambda i: (i, 0))
    return pl.pallas_call(
        _combine_kernel,
        grid=(n // tm,),
        in_specs=[pl.BlockSpec((tm * TOP_K,), lambda i: (i,), memory_space=pltpu.SMEM),
                  row(d), row(LANES), row(p2d.shape[1]), _full(gple.shape), _full(wpg.shape), _full(wpp.shape),
                  pl.BlockSpec(memory_space=pl.ANY)],
        out_specs=row(d),
        out_shape=jax.ShapeDtypeStruct((n, d), F32),
        scratch_shapes=[pltpu.VMEM((TOP_K, tm * (d // LANES), LANES), F32), pltpu.SemaphoreType.DMA(())],
        compiler_params=_cparams(("arbitrary",)),
        name="combine",
    )(dest_flat, x1, gates, p2d, gple, wpg, wpp, ys)


def _rope_tables(seq):
    pos = jnp.arange(seq, dtype=F32)
    inv_freq = ROPE_THETA ** (-jnp.arange(0, MLA_ROPE, 2, dtype=F32) / MLA_ROPE)
    ang = pos[:, None] * inv_freq[None, :]
    c, s = jnp.cos(ang), jnp.sin(ang)
    reps = LANES // MLA_ROPE
    return jnp.tile(jnp.concatenate([c, c], -1), (1, reps)), jnp.tile(jnp.concatenate([-s, s], -1), (1, reps))


def _layer(x2d, p2d, batch, seq, cos, sin, attn_norm, w_in, q_a_norm, w_uq, kv_a_norm, w_ukv,
           mla_q_nope_norm, mla_q_rope_norm, mla_k_nope_norm, mla_k_rope_norm,
           fox_q_norm, fox_k_norm, fox_f_bias, w_o_mla, w_o_fox, w_o,
           ffn_norm, w_router, b_router, w_up, b_up, w_down, b_down, ple_norm, w_ple_gate, w_ple_proj):
    n, d = x2d.shape
    nf = FOX_HEADS * FOX_HD
    row1 = lambda v: v.reshape(1, -1).astype(F32)

    o = np.cumsum([0, MLA_Q_LORA, MLA_KV_LORA, MLA_ROPE, nf, nf, nf, FOX_HEADS, d, d])
    wa = jnp.concatenate([w_in[:, o[0]:o[3]], w_in[:, o[6]:o[7]],
                          jnp.zeros((d, ZA_COLS - int(o[3]) - FOX_HEADS), w_in.dtype)], axis=1).astype(BF16)
    wf = w_in[:, o[3]:o[6]].astype(BF16)
    wg = w_in[:, o[7]:o[9]].astype(BF16)
    za, zf, zg = _inproj(x2d, row1(attn_norm), wa, wf, wg)

    wq3 = w_uq.reshape(MLA_Q_LORA, MLA_HEADS, MLA_NOPE + MLA_ROPE)
    wuq = jnp.concatenate([wq3[:, :, :MLA_NOPE].reshape(MLA_Q_LORA, -1),
                           wq3[:, :, MLA_NOPE:].reshape(MLA_Q_LORA, -1)], axis=1).astype(BF16)
    wkv3 = w_ukv.reshape(MLA_KV_LORA, MLA_HEADS, MLA_NOPE + MLA_V)
    wukv = jnp.concatenate([wkv3[:, :, :MLA_NOPE].reshape(MLA_KV_LORA, -1),
                            wkv3[:, :, MLA_NOPE:].reshape(MLA_KV_LORA, -1)], axis=1).astype(BF16)
    mla_scale = (MLA_NOPE + MLA_ROPE) ** -0.5 * LOG2E
    fox_scale = FOX_HD ** -0.5 * LOG2E
    gkr = jnp.concatenate([mla_k_rope_norm.astype(F32), jnp.zeros((LANES - MLA_ROPE,), F32)])
    fb = jnp.zeros((LANES,), F32).at[FLOGIT_LANE:FLOGIT_LANE + FOX_HEADS].set(fox_f_bias.astype(F32))
    gains = [row1(q_a_norm), row1(kv_a_norm),
             row1(jnp.tile(mla_q_nope_norm, MLA_HEADS) * mla_scale),
             row1(jnp.tile(mla_q_rope_norm, MLA_HEADS) * mla_scale),
             row1(jnp.tile(mla_k_nope_norm, MLA_HEADS)), row1(gkr),
             row1(jnp.tile(fox_q_norm, FOX_HEADS) * fox_scale), row1(jnp.tile(fox_k_norm, FOX_HEADS)), row1(fb)]
    qm, km, vm, qf, kf, vf = _prep(za, zf, cos, sin, wuq, wukv, gains, batch, seq)

    o_mla = _attention(qm, km, vm, hpb=4, dv=MLA_V).reshape(n, MLA_HEADS * MLA_V)
    o_fox = _attention(qf, kf, vf, hpb=4, dv=FOX_HD).reshape(n, nf)

    wr = jnp.concatenate([w_router.astype(F32), jnp.zeros((d, LANES - N_EXPERTS), F32)], axis=1)
    br = jnp.concatenate([b_router.astype(F32), jnp.zeros((LANES - N_EXPERTS,), F32)]).reshape(1, LANES)
    x1, h2, idx, gates = _outproj(x2d, o_mla, o_fox, zg, w_o_mla.astype(BF16), w_o_fox.astype(BF16),
                                  w_o.astype(BF16), row1(ffn_norm), wr, br)

    n_blocks = n * TOP_K // MOE_ROWS + N_EXPERTS
    dest, be, meta = _route(idx, n_blocks)
    dest_flat = dest[:, :TOP_K].reshape(-1)
    xs = _dispatch(h2, dest_flat, meta, n_blocks * MOE_ROWS, d // LANES)
    ys = _experts(xs, be[:n_blocks, 0], w_up, b_up, w_down, b_down)
    return _combine(x1, gates, dest_flat, p2d, row1(ple_norm), w_ple_gate.astype(BF16),
                    w_ple_proj.astype(BF16), ys)


def kernel(x, p, attn_norm, w_in, q_a_norm, w_uq, kv_a_norm, w_ukv, mla_q_nope_norm, mla_q_rope_norm,
           mla_k_nope_norm, mla_k_rope_norm, fox_q_norm, fox_k_norm, fox_f_bias, w_o_mla, w_o_fox, w_o,
           ffn_norm, w_router, b_router, w_up, b_up, w_down, b_down, ple_norm, w_ple_gate, w_ple_proj):
    batch, seq, d = x.shape
    depth = p.shape[0]
    cos, sin = _rope_tables(seq)
    x2d = x.reshape(batch * seq, d)
    for i in range(depth):
        x2d = _layer(x2d, p[i].reshape(batch * seq, -1), batch, seq, cos, sin,
                     attn_norm[i], w_in[i], q_a_norm[i], w_uq[i], kv_a_norm[i], w_ukv[i],
                     mla_q_nope_norm[i], mla_q_rope_norm[i], mla_k_nope_norm[i], mla_k_rope_norm[i],
                     fox_q_norm[i], fox_k_norm[i], fox_f_bias[i], w_o_mla[i], w_o_fox[i], w_o[i],
                     ffn_norm[i], w_router[i], b_router[i], w_up[i], b_up[i], w_down[i], b_down[i],
                     ple_norm[i], w_ple_gate[i], w_ple_proj[i])
    return x2d.reshape(batch, seq, d)
```

```python
import functools

import numpy as np
import jax
import jax.numpy as jnp
from jax import lax
from jax.experimental import pallas as pl
from jax.experimental.pallas import tpu as pltpu

F32 = jnp.float32
BF16 = jnp.bfloat16
I32 = jnp.int32

EPS = 1e-6
ROPE_THETA = 10000.0
MLA_HEADS = 8
MLA_NOPE = 128
MLA_ROPE = 64
MLA_V = 128
MLA_Q_LORA = 256
MLA_KV_LORA = 256
FOX_HEADS = 16
FOX_HD = 64
N_EXPERTS = 32
TOP_K = 4
SWIGLU_LIMIT = 7.0
SWIGLU_ALPHA = 1.702

LANES = 128
ZA_COLS = 640
FLOGIT_LANE = 64
FOX_DK = 128
V_PAD = 16
MOE_ROWS = 256
NEG_BIG = -1e30
LOG2E = 1.4426950408889634
VMEM_LIMIT = 56 * 1024 * 1024

TM_PROJ = 512
TM_DMA = 256
ATTN_TQ = 512


def _cparams(sem):
    return pltpu.CompilerParams(dimension_semantics=sem, vmem_limit_bytes=VMEM_LIMIT)


def _full(shape):
    nd = len(shape)
    return pl.BlockSpec(shape, lambda *_: (0,) * nd)


def _rms_rows(x, gain):
    return x * lax.rsqrt(jnp.mean(x * x, axis=-1, keepdims=True) + EPS) * gain


def _split3(x):
    hi = x.astype(BF16)
    r1 = x - hi.astype(F32)
    mid = r1.astype(BF16)
    lo = (r1 - mid.astype(F32)).astype(BF16)
    return hi, mid, lo


def _dot(a, b):
    return jnp.dot(a, b, preferred_element_type=F32)


def _store_token_tiles(ref, x):
    nch = x.shape[1] // LANES
    for c in range(nch):
        ref[pl.ds(c, x.shape[0], stride=nch), :] = x[:, c * LANES:(c + 1) * LANES]


def _load_token_tiles(ref, rows, nch):
    return jnp.concatenate([ref[pl.ds(c, rows, stride=nch), :] for c in range(nch)], axis=-1)


def _inproj_kernel(x_ref, g_ref, wa_ref, wf_ref, wg_ref, za_ref, zf_ref, zg_ref):
    hb = _rms_rows(x_ref[...], g_ref[...]).astype(BF16)
    za_ref[...] = _dot(hb, wa_ref[...])
    zf_ref[...] = _dot(hb, wf_ref[...]).astype(BF16)
    zg_ref[...] = jax.nn.sigmoid(_dot(hb, wg_ref[...])).astype(BF16)


def _inproj(x2d, gain, wa, wf, wg):
    n, d = x2d.shape
    tm = TM_PROJ
    row = lambda c: pl.BlockSpec((tm, c), lambda i: (i, 0))
    return pl.pallas_call(
        _inproj_kernel,
        grid=(n // tm,),
        in_specs=[row(d), _full(gain.shape), _full(wa.shape), _full(wf.shape), _full(wg.shape)],
        out_specs=[row(wa.shape[1]), row(wf.shape[1]), row(wg.shape[1])],
        out_shape=[jax.ShapeDtypeStruct((n, wa.shape[1]), F32),
                   jax.ShapeDtypeStruct((n, wf.shape[1]), BF16),
                   jax.ShapeDtypeStruct((n, wg.shape[1]), BF16)],
        compiler_params=_cparams(("arbitrary",)),
        name="inproj",
    )(x2d, gain, wa, wf, wg)


def _group_rms(x, gmat, gmat_t, group, gain):
    ss = _dot((x * x).astype(BF16), gmat)
    inv = lax.rsqrt(ss * (1.0 / group) + EPS)
    inv_hi = inv.astype(BF16)
    inv_lo = (inv - inv_hi.astype(F32)).astype(BF16)
    inv_full = _dot(inv_hi, gmat_t) + _dot(inv_lo, gmat_t)
    return x * inv_full * gain


def _rot_half(x):
    c = x.shape[-1]
    lane = lax.broadcasted_iota(I32, x.shape, 1)
    first = (lane & 63) < 32
    return jnp.where(first, pltpu.roll(x, c - 32, 1), pltpu.roll(x, 32, 1))


def _log_sigmoid(x):
    return jnp.minimum(x, 0.0) - jnp.log1p(jnp.exp(-jnp.abs(x)))


def _prep_kernel(za_ref, zf_ref, cos_ref, sin_ref, wuq_ref, wukv_ref,
                 gqa_ref, gkva_ref, gqn_ref, gqr_ref, gkn_ref, gkr_ref, gfq_ref, gfk_ref, fb_ref,
                 g128_ref, g128t_ref, g64_ref, g64t_ref, tri_ref, eq_ref, ek_ref, oneq_ref, onek_ref,
                 qm_ref, km_ref, vm_ref, qf_ref, kf_ref, vf_ref, carry_ref):
    si = pl.program_id(1)
    za = za_ref[...]
    cos = cos_ref[...]
    sin = sin_ref[...]
    g128, g128t = g128_ref[...], g128t_ref[...]
    g64, g64t = g64_ref[...], g64t_ref[...]

    cq = _rms_rows(za[:, 0:MLA_Q_LORA], gqa_ref[...]).astype(BF16)
    ckv = _rms_rows(za[:, MLA_Q_LORA:MLA_Q_LORA + MLA_KV_LORA], gkva_ref[...]).astype(BF16)
    last = za[:, ZA_COLS - LANES:ZA_COLS]
    q = _dot(cq, wuq_ref[...])
    kv = _dot(ckv, wukv_ref[...])
    n_nope = MLA_HEADS * MLA_NOPE
    n_rope = MLA_HEADS * MLA_ROPE
    qn = _group_rms(q[:, :n_nope], g128, g128t, MLA_NOPE, gqn_ref[...])
    qr = _group_rms(q[:, n_nope:], g64[:n_rope], g64t[:, :n_rope], MLA_ROPE, gqr_ref[...])
    cos4 = jnp.concatenate([cos] * (n_rope // LANES), axis=-1)
    sin4 = jnp.concatenate([sin] * (n_rope // LANES), axis=-1)
    qr = qr * cos4 + _rot_half(qr) * sin4
    kn = _group_rms(kv[:, :n_nope], g128, g128t, MLA_NOPE, gkn_ref[...])
    lane = lax.broadcasted_iota(I32, last.shape, 1)
    kr_ss = jnp.sum(jnp.where(lane < MLA_ROPE, last * last, 0.0), axis=-1, keepdims=True)
    kr = last * lax.rsqrt(kr_ss * (1.0 / MLA_ROPE) + EPS) * gkr_ref[...]
    kr = kr * cos + _rot_half(kr) * sin
    kr_b = kr[:, :MLA_ROPE].astype(BF16)
    tm = za.shape[0]
    ones_rows = jnp.where(lax.broadcasted_iota(I32, (V_PAD, tm), 0) == 0, 1.0, 0.0).astype(BF16)
    qn_t = qn.T.astype(BF16)
    qr_t = qr.T.astype(BF16)
    v_t = kv[:, n_nope:].T.astype(BF16)
    for h in range(MLA_HEADS):
        qm_ref[0, h, 0:MLA_NOPE, :] = qn_t[h * MLA_NOPE:(h + 1) * MLA_NOPE, :]
        qm_ref[0, h, MLA_NOPE:MLA_NOPE + MLA_ROPE, :] = qr_t[h * MLA_ROPE:(h + 1) * MLA_ROPE, :]
        km_ref[0, h, :, 0:MLA_NOPE] = kn[:, h * MLA_NOPE:(h + 1) * MLA_NOPE].astype(BF16)
        km_ref[0, h, :, MLA_NOPE:MLA_NOPE + MLA_ROPE] = kr_b
        vm_ref[0, h, 0:MLA_V, :] = v_t[h * MLA_V:(h + 1) * MLA_V, :]
        vm_ref[0, h, MLA_V:MLA_V + V_PAD, :] = ones_rows

    nf = FOX_HEADS * FOX_HD
    zf = zf_ref[...]
    fq = _group_rms(zf[:, 0:nf].astype(F32), g64, g64t, FOX_HD, gfq_ref[...])
    fk = _group_rms(zf[:, nf:2 * nf].astype(F32), g64, g64t, FOX_HD, gfk_ref[...])
    fv = zf[:, 2 * nf:3 * nf]

    @pl.when(si == 0)
    def _():
        carry_ref[...] = jnp.zeros_like(carry_ref)

    logf = _log_sigmoid(last + fb_ref[...])
    l_hi, l_mid, l_lo = _split3(logf)
    tri = tri_ref[...]
    cum = carry_ref[...] + (_dot(tri, l_hi) + _dot(tri, l_mid) + _dot(tri, l_lo))
    tm = cum.shape[0]
    carry_ref[...] = cum[tm - 1:tm, :]
    fcat = jnp.concatenate(_split3(cum * LOG2E), axis=-1)
    augq = _dot(fcat, eq_ref[...]) + oneq_ref[...]
    augk = _dot(fcat, ek_ref[...]) + onek_ref[...]
    fq_t = fq.T.astype(BF16)
    augq_t = augq.T.astype(BF16)
    fv_t = fv.astype(F32).T.astype(BF16)
    for h in range(FOX_HEADS):
        sl = slice(h * FOX_HD, (h + 1) * FOX_HD)
        qf_ref[0, h, 0:FOX_HD, :] = fq_t[sl, :]
        qf_ref[0, h, FOX_HD:2 * FOX_HD, :] = augq_t[sl, :]
        kf_ref[0, h] = jnp.concatenate([fk[:, sl], augk[:, sl]], axis=-1).astype(BF16)
        vf_ref[0, h, 0:FOX_HD, :] = fv_t[sl, :]
        vf_ref[0, h, FOX_HD:FOX_HD + V_PAD, :] = ones_rows


def _group_matrix(cols, group):
    m = np.zeros((cols, LANES), np.float32)
    m[np.arange(cols), np.arange(cols) // group] = 1.0
    return m


def _bias_column_matrices():
    nf = FOX_HEADS * FOX_HD
    eq = np.zeros((3 * LANES, nf), np.float32)
    ek = np.zeros((3 * LANES, nf), np.float32)
    oneq = np.zeros((1, nf), np.float32)
    onek = np.zeros((1, nf), np.float32)
    for h in range(FOX_HEADS):
        for t in range(3):
            eq[t * LANES + FLOGIT_LANE + h, h * FOX_HD + t] = 1.0
            ek[t * LANES + FLOGIT_LANE + h, h * FOX_HD + 3 + t] = -1.0
            oneq[0, h * FOX_HD + 3 + t] = 1.0
            onek[0, h * FOX_HD + t] = 1.0
    return eq, ek, oneq, onek


def _prep(za, zf, cos, sin, wuq, wukv, gains, batch, seq):
    tm = TM_PROJ
    nt = seq // tm
    g128 = _group_matrix(MLA_HEADS * MLA_NOPE, MLA_NOPE)
    g64 = _group_matrix(FOX_HEADS * FOX_HD, FOX_HD)
    tri = np.tril(np.ones((tm, tm), np.float32))
    eq, ek, oneq, onek = _bias_column_matrices()
    consts = [jnp.asarray(g128, BF16), jnp.asarray(g128.T, BF16), jnp.asarray(g64, BF16), jnp.asarray(g64.T, BF16),
              jnp.asarray(tri, BF16), jnp.asarray(eq, BF16), jnp.asarray(ek, BF16),
              jnp.asarray(oneq, F32), jnp.asarray(onek, F32)]
    row = lambda c: pl.BlockSpec((tm, c), lambda b, s: (b * nt + s, 0))
    tab = pl.BlockSpec((tm, LANES), lambda b, s: (s, 0))
    head = lambda h, d: pl.BlockSpec((1, h, tm, d), lambda b, s: (b, 0, s, 0))
    head_t = lambda h, d: pl.BlockSpec((1, h, d, tm), lambda b, s: (b, 0, 0, s))
    hshape = lambda h, d: jax.ShapeDtypeStruct((batch, h, seq, d), BF16)
    hshape_t = lambda h, d: jax.ShapeDtypeStruct((batch, h, d, seq), BF16)
    dk = MLA_NOPE + MLA_ROPE
    return pl.pallas_call(
        _prep_kernel,
        grid=(batch, nt),
        in_specs=[row(za.shape[1]), row(zf.shape[1]), tab, tab, _full(wuq.shape), _full(wukv.shape)]
                 + [_full(g.shape) for g in gains] + [_full(c.shape) for c in consts],
        out_specs=[head_t(MLA_HEADS, dk), head(MLA_HEADS, dk), head_t(MLA_HEADS, MLA_V + V_PAD),
                   head_t(FOX_HEADS, FOX_DK), head(FOX_HEADS, FOX_DK), head_t(FOX_HEADS, FOX_HD + V_PAD)],
        out_shape=[hshape_t(MLA_HEADS, dk), hshape(MLA_HEADS, dk), hshape_t(MLA_HEADS, MLA_V + V_PAD),
                   hshape_t(FOX_HEADS, FOX_DK), hshape(FOX_HEADS, FOX_DK), hshape_t(FOX_HEADS, FOX_HD + V_PAD)],
        scratch_shapes=[pltpu.VMEM((1, LANES), F32)],
        compiler_params=_cparams(("arbitrary", "arbitrary")),
        name="prep",
    )(za, zf, cos, sin, wuq, wukv, *gains, *consts)


def _attn_kernel(qt_ref, k_ref, vt_ref, o_ref, *scratch, tq, tk, hpb, dv):
    qi = pl.program_id(2)
    m_refs, acc_refs, s_refs = scratch[:hpb], scratch[hpb:2 * hpb], scratch[2 * hpb:]
    n_ahead = len(s_refs)

    def score(hh, j):
        return _dot(k_ref[0, hh, pl.ds(pl.multiple_of(j * tk, tk), tk), :], qt_ref[0, hh])

    def kv_block(j, diagonal):
        start = pl.multiple_of(j * tk, tk)
        state = [(m_refs[hh][...], acc_refs[hh][...]) for hh in range(hpb)]
        ready = {hh: s_refs[hh][...] for hh in range(n_ahead)}
        upcoming = list(range(n_ahead, hpb)) + ([] if diagonal else [("next", hh) for hh in range(n_ahead)])

        def issue_next_score():
            if not upcoming:
                return
            item = upcoming.pop(0)
            if isinstance(item, tuple):
                s_refs[item[1]][...] = score(item[1], j + 1)
            else:
                ready[item] = score(item, j)

        issue_next_score()
        results = []
        for hh in range(hpb):
            s, (m_old, acc_old) = ready.pop(hh), state[hh]
            if diagonal:
                key = lax.broadcasted_iota(I32, s.shape, 0)
                qry = lax.broadcasted_iota(I32, s.shape, 1)
                s = jnp.where(key <= qry, s, NEG_BIG)
            m_new = jnp.maximum(m_old, jnp.max(s, axis=0, keepdims=True))
            p = jnp.exp2((s - m_new).astype(BF16))
            pv = _dot(vt_ref[0, hh, :, pl.ds(start, tk)], p)
            results.append((m_new, acc_old * jnp.exp2(m_old - m_new) + pv))
            issue_next_score()
        while upcoming:
            issue_next_score()
        for hh in range(hpb):
            m_refs[hh][...] = results[hh][0]
            acc_refs[hh][...] = results[hh][1]

    for hh in range(hpb):
        m_refs[hh][...] = jnp.full(m_refs[hh].shape, NEG_BIG, F32)
        acc_refs[hh][...] = jnp.zeros(acc_refs[hh].shape, F32)
    for hh in range(n_ahead):
        s_refs[hh][...] = score(hh, 0)

    def full_block(j, carry):
        kv_block(j, False)
        return carry

    lax.fori_loop(0, qi, full_block, 0)
    kv_block(qi, True)
    outs = []
    for hh in range(hpb):
        acc = acc_refs[hh][...]
        outs.append(acc[:dv, :] / acc[dv:dv + 1, :])
    o_ref[0] = jnp.concatenate(outs, axis=0).T.astype(o_ref.dtype)


def _attention(qt, k, vt, hpb, dv):
    b, h, s, dk = k.shape
    dvp = vt.shape[2]
    tq = tk = ATTN_TQ
    assert s % tq == 0 and h % hpb == 0 and (hpb * dv) % LANES == 0 and dvp == dv + V_PAD
    kern = functools.partial(_attn_kernel, tq=tq, tk=tk, hpb=hpb, dv=dv)
    kv_bytes = hpb * s * (-(-dk // LANES) * LANES + dvp) * 2
    kv_mode = {} if 2 * kv_bytes <= VMEM_LIMIT // 2 else {"pipeline_mode": pl.Buffered(1)}
    return pl.pallas_call(
        kern,
        grid=(b, h // hpb, s // tq),
        in_specs=[pl.BlockSpec((1, hpb, dk, tq), lambda bi, hi, qi: (bi, hi, 0, qi)),
                  pl.BlockSpec((1, hpb, s, dk), lambda bi, hi, qi: (bi, hi, 0, 0), **kv_mode),
                  pl.BlockSpec((1, hpb, dvp, s), lambda bi, hi, qi: (bi, hi, 0, 0), **kv_mode)],
        out_specs=pl.BlockSpec((1, tq, hpb * dv), lambda bi, hi, qi: (bi, qi, hi)),
        out_shape=jax.ShapeDtypeStruct((b, s, h * dv), BF16),
        scratch_shapes=([pltpu.VMEM((1, tq), F32)] * hpb + [pltpu.VMEM((dvp, tq), F32)] * hpb
                        + [pltpu.VMEM((tk, tq), F32)] * min(2, hpb)),
        compiler_params=_cparams(("arbitrary", "arbitrary", "arbitrary")),
        name="attention",
    )(qt, k, vt)


def _outproj_kernel(x_ref, om_ref, of_ref, zg_ref, wom_ref, wof_ref, wo_ref, gffn_ref, wr_ref, br_ref,
                    x1_ref, h2_ref, idx_ref, gate_ref):
    d = x_ref.shape[1]
    zg = zg_ref[...].astype(F32)
    mixed = zg[:, :d] * _dot(om_ref[...], wom_ref[...]) + zg[:, d:] * _dot(of_ref[...], wof_ref[...])
    x1 = x_ref[...] + _dot(mixed.astype(BF16), wo_ref[...])
    x1_ref[...] = x1
    h2 = _rms_rows(x1, gffn_ref[...])
    _store_token_tiles(h2_ref, h2)
    h_hi = h2.astype(BF16)
    h_lo = (h2 - h_hi.astype(F32)).astype(BF16)
    hi_terms = _dot(h_hi, wr_ref[...])
    logits = hi_terms[:, :LANES] + hi_terms[:, LANES:] + _dot(h_lo, wr_ref[:, :LANES]) + br_ref[...]
    lane = lax.broadcasted_iota(I32, logits.shape, 1)
    lane_f = lane.astype(F32)
    work = jnp.where(lane < N_EXPERTS, logits, -jnp.inf)
    vals, idxs = [], []
    for _ in range(TOP_K):
        m = jnp.max(work, axis=-1, keepdims=True)
        idx = jnp.min(jnp.where(work == m, lane_f, float(LANES)), axis=-1, keepdims=True)
        vals.append(m)
        idxs.append(idx)
        work = jnp.where(lane_f == idx, -jnp.inf, work)
    exps = [jnp.exp(v - vals[0]) for v in vals]
    denom = exps[0] + exps[1] + exps[2] + exps[3]
    idx_out = jnp.zeros(logits.shape, F32)
    gate_out = jnp.zeros(logits.shape, F32)
    for kk in range(TOP_K):
        idx_out = jnp.where(lane == kk, idxs[kk], idx_out)
        gate_out = jnp.where(lane == kk, exps[kk] / denom, gate_out)
    idx_ref[...] = idx_out.astype(I32)
    gate_ref[...] = gate_out


def _outproj(x2d, o_mla, o_fox, zg, wom, wof, wo, gffn, wr, br):
    n, d = x2d.shape
    tm = TM_PROJ
    row = lambda c: pl.BlockSpec((tm, c), lambda i: (i, 0))
    return pl.pallas_call(
        _outproj_kernel,
        grid=(n // tm,),
        in_specs=[row(d), row(o_mla.shape[1]), row(o_fox.shape[1]), row(zg.shape[1]),
                  _full(wom.shape), _full(wof.shape), _full(wo.shape), _full(gffn.shape),
                  _full(wr.shape), _full(br.shape)],
        out_specs=[row(d), pl.BlockSpec((tm * (d // LANES), LANES), lambda i: (i, 0)), row(LANES), row(LANES)],
        out_shape=[jax.ShapeDtypeStruct((n, d), F32), jax.ShapeDtypeStruct((n * (d // LANES), LANES), F32),
                   jax.ShapeDtypeStruct((n, LANES), I32), jax.ShapeDtypeStruct((n, LANES), F32)],
        compiler_params=_cparams(("arbitrary",)),
        name="outproj",
    )(x2d, o_mla, o_fox, zg, wom, wof, wo, gffn, wr, br)


def _route_kernel(idx_ref, lstrict_ref, ustrict_ref, dest_ref, be_ref, meta_ref, cnt_ref, start_ref, run_ref):
    phase = pl.program_id(0)
    i = pl.program_id(1)
    idx = idx_ref[...]
    lane = lax.broadcasted_iota(I32, idx.shape, 1)
    hit = [lane == idx[:, kk:kk + 1] for kk in range(TOP_K)]
    onehot = jnp.zeros(idx.shape, F32)
    for hk in hit:
        onehot = jnp.where(hk, 1.0, onehot)
    tile_cnt = jnp.sum(onehot, axis=0, keepdims=True)

    @pl.when((phase == 0) & (i == 0))
    def _():
        cnt_ref[...] = jnp.zeros_like(cnt_ref)

    @pl.when(phase == 0)
    def _():
        cnt_ref[...] += tile_cnt

    @pl.when((phase == 1) & (i == 0))
    def _():
        cnt = cnt_ref[...]
        nblk = jnp.floor((cnt + (MOE_ROWS - 1.0)) * (1.0 / MOE_ROWS))
        nblk8 = jnp.broadcast_to(nblk, (8, LANES)).astype(BF16)
        start = _dot(nblk8, ustrict_ref[...])[0:1, :]
        start_ref[...] = start
        run_ref[...] = jnp.zeros_like(run_ref)
        end = start + nblk
        nb = be_ref.shape[0]
        bid = lax.broadcasted_iota(I32, (nb, LANES), 0).astype(F32)
        lane_b = lax.broadcasted_iota(I32, (nb, LANES), 1)
        owned = jnp.where((lane_b < N_EXPERTS) & (end <= bid), 1.0, 0.0)
        be = jnp.minimum(jnp.sum(owned, axis=-1, keepdims=True), N_EXPERTS - 1.0)
        be_ref[...] = jnp.broadcast_to(be, (nb, LANES)).astype(I32)
        pad_lo = start * MOE_ROWS + cnt
        pad_hi = end * MOE_ROWS
        row8 = lax.broadcasted_iota(I32, (8, LANES), 0)
        meta = jnp.where(row8 == 0, pad_lo, jnp.where(row8 == 1, pad_hi, jnp.where(row8 == 2, end, 0.0)))
        meta_ref[...] = meta.astype(I32)

    @pl.when(phase == 1)
    def _():
        rank = _dot(lstrict_ref[...], onehot.astype(BF16))
        slot = start_ref[...] * MOE_ROWS + run_ref[...] + rank
        out = jnp.zeros(idx.shape, F32)
        for kk in range(TOP_K):
            dk = jnp.sum(jnp.where(hit[kk], slot, 0.0), axis=-1, keepdims=True)
            out = jnp.where(lane == kk, dk, out)
        dest_ref[...] = out.astype(I32)
        run_ref[...] += tile_cnt


def _route(idx, n_blocks):
    n = idx.shape[0]
    tm = TM_PROJ
    nt = n // tm
    nb = (n_blocks + 7) // 8 * 8
    lstrict = jnp.asarray(np.tril(np.ones((tm, tm), np.float32), -1), BF16)
    ustrict = jnp.asarray(np.triu(np.ones((LANES, LANES), np.float32), 1), BF16)
    return pl.pallas_call(
        _route_kernel,
        grid=(2, nt),
        in_specs=[pl.BlockSpec((tm, LANES), lambda p, i: (i, 0)), _full(lstrict.shape), _full(ustrict.shape)],
        out_specs=[pl.BlockSpec((tm, LANES), lambda p, i: (i * p, 0)), _full((nb, LANES)), _full((8, LANES))],
        out_shape=[jax.ShapeDtypeStruct((n, LANES), I32), jax.ShapeDtypeStruct((nb, LANES), I32),
                   jax.ShapeDtypeStruct((8, LANES), I32)],
        scratch_shapes=[pltpu.VMEM((1, LANES), F32), pltpu.VMEM((1, LANES), F32), pltpu.VMEM((1, LANES), F32)],
        compiler_params=_cparams(("arbitrary", "arbitrary")),
        name="route",
    )(idx, lstrict, ustrict)


def _dispatch_kernel(meta_ref, dest_ref, h_ref, xs_ref, zero_ref, sem, zsem, *, nch):
    i = pl.program_id(0)
    tm = h_ref.shape[0] // nch

    def tile(ref, r):
        return ref.at[pl.ds(pl.multiple_of(r * nch, nch), nch), :]

    def row_copy(t, kk):
        return pltpu.make_async_copy(tile(h_ref, t), tile(xs_ref, dest_ref[t * TOP_K + kk]), sem)

    def issue(t, carry):
        for kk in range(TOP_K):
            row_copy(t, kk).start(priority=kk % 2)
        return carry

    def drain(t, carry):
        for kk in range(TOP_K):
            row_copy(t, kk).wait()
        return carry

    lax.fori_loop(0, tm, issue, 0)

    @pl.when(i == pl.num_programs(0) - 1)
    def _():
        zero_ref[...] = jnp.zeros_like(zero_ref)
        blk = MOE_ROWS * nch

        def pad_copy(r):
            return pltpu.make_async_copy(tile(zero_ref, 0), tile(xs_ref, r), zsem)

        def blk_copy(b):
            return pltpu.make_async_copy(zero_ref, xs_ref.at[pl.ds(pl.multiple_of(b * blk, blk), blk), :], zsem)

        for e in range(N_EXPERTS):
            lo, hi = meta_ref[0, e], meta_ref[1, e]
            lax.fori_loop(lo, hi, lambda r, c: (pad_copy(r).start(), c)[1], 0)
            lax.fori_loop(lo, hi, lambda r, c: (pad_copy(r).wait(), c)[1], 0)
        n_used = meta_ref[2, N_EXPERTS - 1]
        n_blocks = xs_ref.shape[0] // blk
        lax.fori_loop(n_used, n_blocks, lambda b, c: (blk_copy(b).start(), c)[1], 0)
        lax.fori_loop(n_used, n_blocks, lambda b, c: (blk_copy(b).wait(), c)[1], 0)

    lax.fori_loop(0, tm, drain, 0)


def _dispatch(h2t, dest_flat, meta, n_rows, nch):
    n = h2t.shape[0] // nch
    tm = TM_DMA
    grid_spec = pltpu.PrefetchScalarGridSpec(
        num_scalar_prefetch=1,
        grid=(n // tm,),
        in_specs=[pl.BlockSpec((tm * TOP_K,), lambda i, meta: (i,), memory_space=pltpu.SMEM),
                  pl.BlockSpec((tm * nch, LANES), lambda i, meta: (i, 0))],
        out_specs=pl.BlockSpec(memory_space=pl.ANY),
        scratch_shapes=[pltpu.VMEM((MOE_ROWS * nch, LANES), F32), pltpu.SemaphoreType.DMA(()),
                        pltpu.SemaphoreType.DMA(())],
    )
    return pl.pallas_call(
        functools.partial(_dispatch_kernel, nch=nch),
        grid_spec=grid_spec,
        out_shape=jax.ShapeDtypeStruct((n_rows * nch, LANES), F32),
        compiler_params=_cparams(("arbitrary",)),
        name="dispatch",
    )(meta, dest_flat, h2t)


def _experts_kernel(be_ref, xs_ref, wu_ref, bu_ref, wd_ref, bd_ref, y_ref, wu_bf, wd_bf):
    b = pl.program_id(0)
    ff = wd_ref.shape[1]
    prev = be_ref[jnp.maximum(b - 1, 0)]

    @pl.when((b == 0) | (be_ref[b] != prev))
    def _():
        wu_bf[...] = wu_ref[0].astype(BF16)
        wd_bf[...] = wd_ref[0].astype(BF16)

    nch = wu_ref.shape[1] // LANES
    x = _load_token_tiles(xs_ref, xs_ref.shape[0] // nch, nch)
    gu = _dot(x.astype(BF16), wu_bf[...]) + bu_ref[0]
    g = jnp.minimum(gu[:, :ff], SWIGLU_LIMIT)
    u = jnp.clip(gu[:, ff:], -SWIGLU_LIMIT, SWIGLU_LIMIT)
    act = g * jax.nn.sigmoid(SWIGLU_ALPHA * g) * (u + 1.0)
    _store_token_tiles(y_ref, _dot(act.astype(BF16), wd_bf[...]) + bd_ref[0])


def _experts(xs, block_expert, w_up, b_up, w_down, b_down):
    e, d, ff2 = w_up.shape
    ff = w_down.shape[1]
    nch = d // LANES
    n_rows = xs.shape[0] // nch
    grid_spec = pltpu.PrefetchScalarGridSpec(
        num_scalar_prefetch=1,
        grid=(n_rows // MOE_ROWS,),
        in_specs=[pl.BlockSpec((MOE_ROWS * nch, LANES), lambda b, be: (b, 0)),
                  pl.BlockSpec((1, d, ff2), lambda b, be: (be[b], 0, 0)),
                  pl.BlockSpec((1, 1, ff2), lambda b, be: (be[b], 0, 0)),
                  pl.BlockSpec((1, ff, d), lambda b, be: (be[b], 0, 0)),
                  pl.BlockSpec((1, 1, d), lambda b, be: (be[b], 0, 0))],
        out_specs=pl.BlockSpec((MOE_ROWS * nch, LANES), lambda b, be: (b, 0)),
        scratch_shapes=[pltpu.VMEM((d, ff2), BF16), pltpu.VMEM((ff, d), BF16)],
    )
    return pl.pallas_call(
        _experts_kernel,
        grid_spec=grid_spec,
        out_shape=jax.ShapeDtypeStruct((n_rows * nch, LANES), F32),
        compiler_params=_cparams(("arbitrary",)),
        name="experts",
    )(block_expert, xs, w_up, b_up.reshape(e, 1, ff2), w_down, b_down.reshape(e, 1, d))


def _combine_kernel(dest_ref, dest_next_ref, x1_ref, gate_ref, p_ref, gple_ref, wpg_ref, wpp_ref, ys_ref, o_ref,
                    rows_ref, sem):
    i = pl.program_id(0)
    tm, d = x1_ref.shape
    nch = d // LANES
    slot = lax.rem(i, 2)

    def row_copy(idx_ref, buf, t, kk):
        src = pl.multiple_of(idx_ref[t * TOP_K + kk] * nch, nch)
        dst = pl.multiple_of(t * nch, nch)
        return pltpu.make_async_copy(ys_ref.at[pl.ds(src, nch), :], rows_ref.at[buf, kk, pl.ds(dst, nch), :],
                                     sem.at[buf])

    def gather(idx_ref, buf):
        def issue(t, carry):
            for kk in range(TOP_K):
                row_copy(idx_ref, buf, t, kk).start(priority=kk % 2)
            return carry
        lax.fori_loop(0, tm, issue, 0)

    @pl.when(i == 0)
    def _():
        gather(dest_ref, 0)

    @pl.when(i + 1 < pl.num_programs(0))
    def _():
        gather(dest_next_ref, 1 - slot)

    proj = _dot(p_ref[...].astype(BF16), wpp_ref[...])

    def drain(t, carry):
        for kk in range(TOP_K):
            row_copy(dest_ref, slot, t, kk).wait()
        return carry

    lax.fori_loop(0, tm, drain, 0)
    gates = gate_ref[...]
    moe = gates[:, 0:1] * _load_token_tiles(rows_ref.at[slot, 0], tm, nch)
    for kk in range(1, TOP_K):
        moe = moe + gates[:, kk:kk + 1] * _load_token_tiles(rows_ref.at[slot, kk], tm, nch)
    x2 = x1_ref[...] + moe
    hn = _rms_rows(x2, gple_ref[...]).astype(BF16)
    o_ref[...] = x2 + jax.nn.sigmoid(_dot(hn, wpg_ref[...])) * proj


def _combine(x1, gates, dest_flat, p2d, gple, wpg, wpp, ys):
    n, d = x1.shape
    tm = TM_DMA
    nt = n // tm
    row = lambda c: pl.BlockSpec((tm, c), lambda i: (i, 0))
    return pl.pallas_call(
        _combine_kernel,
        grid=(nt,),
        in_specs=[pl.BlockSpec((tm * TOP_K,), lambda i: (i,), memory_space=pltpu.SMEM),
                  pl.BlockSpec((tm * TOP_K,), lambda i: (jnp.minimum(i + 1, nt - 1),), memory_space=pltpu.SMEM),
                  row(d), row(LANES), row(p2d.shape[1]), _full(gple.shape), _full(wpg.shape), _full(wpp.shape),
                  pl.BlockSpec(memory_space=pl.ANY)],
        out_specs=row(d),
        out_shape=jax.ShapeDtypeStruct((n, d), F32),
        scratch_shapes=[pltpu.VMEM((2, TOP_K, tm * (d // LANES), LANES), F32), pltpu.SemaphoreType.DMA((2,))],
        compiler_params=_cparams(("arbitrary",)),
        name="combine",
    )(dest_flat, dest_flat, x1, gates, p2d, gple, wpg, wpp, ys)


def _rope_tables(seq):
    pos = jnp.arange(seq, dtype=F32)
    inv_freq = ROPE_THETA ** (-jnp.arange(0, MLA_ROPE, 2, dtype=F32) / MLA_ROPE)
    ang = pos[:, None] * inv_freq[None, :]
    c, s = jnp.cos(ang), jnp.sin(ang)
    reps = LANES // MLA_ROPE
    return jnp.tile(jnp.concatenate([c, c], -1), (1, reps)), jnp.tile(jnp.concatenate([-s, s], -1), (1, reps))


def _layer(x2d, p2d, batch, seq, cos, sin, attn_norm, w_in, q_a_norm, w_uq, kv_a_norm, w_ukv,
           mla_q_nope_norm, mla_q_rope_norm, mla_k_nope_norm, mla_k_rope_norm,
           fox_q_norm, fox_k_norm, fox_f_bias, w_o_mla, w_o_fox, w_o,
           ffn_norm, w_router, b_router, w_up, b_up, w_down, b_down, ple_norm, w_ple_gate, w_ple_proj):
    n, d = x2d.shape
    nf = FOX_HEADS * FOX_HD
    row1 = lambda v: v.reshape(1, -1).astype(F32)

    o = np.cumsum([0, MLA_Q_LORA, MLA_KV_LORA, MLA_ROPE, nf, nf, nf, FOX_HEADS, d, d])
    wa = jnp.concatenate([w_in[:, o[0]:o[3]], w_in[:, o[6]:o[7]],
                          jnp.zeros((d, ZA_COLS - int(o[3]) - FOX_HEADS), w_in.dtype)], axis=1).astype(BF16)
    wf = w_in[:, o[3]:o[6]].astype(BF16)
    wg = w_in[:, o[7]:o[9]].astype(BF16)
    za, zf, zg = _inproj(x2d, row1(attn_norm), wa, wf, wg)

    wq3 = w_uq.reshape(MLA_Q_LORA, MLA_HEADS, MLA_NOPE + MLA_ROPE)
    wuq = jnp.concatenate([wq3[:, :, :MLA_NOPE].reshape(MLA_Q_LORA, -1),
                           wq3[:, :, MLA_NOPE:].reshape(MLA_Q_LORA, -1)], axis=1).astype(BF16)
    wkv3 = w_ukv.reshape(MLA_KV_LORA, MLA_HEADS, MLA_NOPE + MLA_V)
    wukv = jnp.concatenate([wkv3[:, :, :MLA_NOPE].reshape(MLA_KV_LORA, -1),
                            wkv3[:, :, MLA_NOPE:].reshape(MLA_KV_LORA, -1)], axis=1).astype(BF16)
    mla_scale = (MLA_NOPE + MLA_ROPE) ** -0.5 * LOG2E
    fox_scale = FOX_HD ** -0.5 * LOG2E
    gkr = jnp.concatenate([mla_k_rope_norm.astype(F32), jnp.zeros((LANES - MLA_ROPE,), F32)])
    fb = jnp.zeros((LANES,), F32).at[FLOGIT_LANE:FLOGIT_LANE + FOX_HEADS].set(fox_f_bias.astype(F32))
    gains = [row1(q_a_norm), row1(kv_a_norm),
             row1(jnp.tile(mla_q_nope_norm, MLA_HEADS) * mla_scale),
             row1(jnp.tile(mla_q_rope_norm, MLA_HEADS) * mla_scale),
             row1(jnp.tile(mla_k_nope_norm, MLA_HEADS)), row1(gkr),
             row1(jnp.tile(fox_q_norm, FOX_HEADS) * fox_scale), row1(jnp.tile(fox_k_norm, FOX_HEADS)), row1(fb)]
    qm, km, vm, qf, kf, vf = _prep(za, zf, cos, sin, wuq, wukv, gains, batch, seq)

    o_mla = _attention(qm, km, vm, hpb=4, dv=MLA_V).reshape(n, MLA_HEADS * MLA_V)
    o_fox = _attention(qf, kf, vf, hpb=4, dv=FOX_HD).reshape(n, nf)

    wr = jnp.concatenate([w_router.astype(F32), jnp.zeros((d, LANES - N_EXPERTS), F32)], axis=1)
    wr_hi = wr.astype(BF16)
    wr = jnp.concatenate([wr_hi, (wr - wr_hi.astype(F32)).astype(BF16)], axis=1)
    br = jnp.concatenate([b_router.astype(F32), jnp.zeros((LANES - N_EXPERTS,), F32)]).reshape(1, LANES)
    x1, h2, idx, gates = _outproj(x2d, o_mla, o_fox, zg, w_o_mla.astype(BF16), w_o_fox.astype(BF16),
                                  w_o.astype(BF16), row1(ffn_norm), wr, br)

    n_blocks = n * TOP_K // MOE_ROWS + N_EXPERTS
    dest, be, meta = _route(idx, n_blocks)
    dest_flat = dest[:, :TOP_K].reshape(-1)
    xs = _dispatch(h2, dest_flat, meta, n_blocks * MOE_ROWS, d // LANES)
    ys = _experts(xs, be[:n_blocks, 0], w_up, b_up, w_down, b_down)
    return _combine(x1, gates, dest_flat, p2d, row1(ple_norm), w_ple_gate.astype(BF16),
                    w_ple_proj.astype(BF16), ys)


def kernel(x, p, attn_norm, w_in, q_a_norm, w_uq, kv_a_norm, w_ukv, mla_q_nope_norm, mla_q_rope_norm,
           mla_k_nope_norm, mla_k_rope_norm, fox_q_norm, fox_k_norm, fox_f_bias, w_o_mla, w_o_fox, w_o,
           ffn_norm, w_router, b_router, w_up, b_up, w_down, b_down, ple_norm, w_ple_gate, w_ple_proj):
    batch, seq, d = x.shape
    depth = p.shape[0]
    cos, sin = _rope_tables(seq)
    x2d = x.reshape(batch * seq, d)
    for i in range(depth):
        x2d = _layer(x2d, p[i].reshape(batch * seq, -1), batch, seq, cos, sin,
                     attn_norm[i], w_in[i], q_a_norm[i], w_uq[i], kv_a_norm[i], w_ukv[i],
                     mla_q_nope_norm[i], mla_q_rope_norm[i], mla_k_nope_norm[i], mla_k_rope_norm[i],
                     fox_q_norm[i], fox_k_norm[i], fox_f_bias[i], w_o_mla[i], w_o_fox[i], w_o[i],
                     ffn_norm[i], w_router[i], b_router[i], w_up[i], b_up[i], w_down[i], b_down[i],
                     ple_norm[i], w_ple_gate[i], w_ple_proj[i])
    return x2d.reshape(batch, seq, d)
```

```python
import functools

import numpy as np
import jax
import jax.numpy as jnp
from jax import lax
from jax.experimental import pallas as pl
from jax.experimental.pallas import tpu as pltpu

F32 = jnp.float32
BF16 = jnp.bfloat16
I32 = jnp.int32

EPS = 1e-6
ROPE_THETA = 10000.0
MLA_HEADS = 8
MLA_NOPE = 128
MLA_ROPE = 64
MLA_V = 128
MLA_Q_LORA = 256
MLA_KV_LORA = 256
FOX_HEADS = 16
FOX_HD = 64
N_EXPERTS = 32
TOP_K = 4
SWIGLU_LIMIT = 7.0
SWIGLU_ALPHA = 1.702

LANES = 128
ZA_COLS = 640
FLOGIT_LANE = 64
FOX_DK = 128
V_PAD = 16
MOE_ROWS = 256
NEG_BIG = -1e30
LOG2E = 1.4426950408889634
VMEM_LIMIT = 56 * 1024 * 1024

TM_PROJ = 512
TM_DMA = 256
ATTN_TQ = 512


def _cparams(sem):
    return pltpu.CompilerParams(dimension_semantics=sem, vmem_limit_bytes=VMEM_LIMIT)


def _full(shape):
    nd = len(shape)
    return pl.BlockSpec(shape, lambda *_: (0,) * nd)


def _rms_rows(x, gain):
    return x * lax.rsqrt(jnp.mean(x * x, axis=-1, keepdims=True) + EPS) * gain


def _split3(x):
    hi = x.astype(BF16)
    r1 = x - hi.astype(F32)
    mid = r1.astype(BF16)
    lo = (r1 - mid.astype(F32)).astype(BF16)
    return hi, mid, lo


def _dot(a, b):
    return jnp.dot(a, b, preferred_element_type=F32)


def _store_token_tiles(ref, x):
    nch = x.shape[1] // LANES
    for c in range(nch):
        ref[pl.ds(c, x.shape[0], stride=nch), :] = x[:, c * LANES:(c + 1) * LANES]


def _load_token_tiles(ref, rows, nch):
    return jnp.concatenate([ref[pl.ds(c, rows, stride=nch), :] for c in range(nch)], axis=-1)


def _inproj_kernel(x_ref, g_ref, wa_ref, wf_ref, wg_ref, za_ref, zf_ref, zg_ref):
    hb = _rms_rows(x_ref[...], g_ref[...]).astype(BF16)
    za_ref[...] = _dot(hb, wa_ref[...])
    zf_ref[...] = _dot(hb, wf_ref[...]).astype(BF16)
    zg_ref[...] = jax.nn.sigmoid(_dot(hb, wg_ref[...])).astype(BF16)


def _inproj(x2d, gain, wa, wf, wg):
    n, d = x2d.shape
    tm = TM_PROJ
    row = lambda c: pl.BlockSpec((tm, c), lambda i: (i, 0))
    return pl.pallas_call(
        _inproj_kernel,
        grid=(n // tm,),
        in_specs=[row(d), _full(gain.shape), _full(wa.shape), _full(wf.shape), _full(wg.shape)],
        out_specs=[row(wa.shape[1]), row(wf.shape[1]), row(wg.shape[1])],
        out_shape=[jax.ShapeDtypeStruct((n, wa.shape[1]), F32),
                   jax.ShapeDtypeStruct((n, wf.shape[1]), BF16),
                   jax.ShapeDtypeStruct((n, wg.shape[1]), BF16)],
        compiler_params=_cparams(("arbitrary",)),
        name="inproj",
    )(x2d, gain, wa, wf, wg)


def _group_rms(x, gmat, gmat_t, group, gain):
    ss = _dot((x * x).astype(BF16), gmat)
    inv = lax.rsqrt(ss * (1.0 / group) + EPS)
    inv_hi = inv.astype(BF16)
    inv_lo = (inv - inv_hi.astype(F32)).astype(BF16)
    inv_full = _dot(inv_hi, gmat_t) + _dot(inv_lo, gmat_t)
    return x * inv_full * gain


def _rot_half(x):
    c = x.shape[-1]
    lane = lax.broadcasted_iota(I32, x.shape, 1)
    first = (lane & 63) < 32
    return jnp.where(first, pltpu.roll(x, c - 32, 1), pltpu.roll(x, 32, 1))


def _log_sigmoid(x):
    return jnp.minimum(x, 0.0) - jnp.log1p(jnp.exp(-jnp.abs(x)))


def _prep_kernel(za_ref, zf_ref, cos_ref, sin_ref, wuq_ref, wukv_ref,
                 gqa_ref, gkva_ref, gqn_ref, gqr_ref, gkn_ref, gkr_ref, gfq_ref, gfk_ref, fb_ref,
                 g128_ref, g128t_ref, g64_ref, g64t_ref, tri_ref, eq_ref, ek_ref, oneq_ref, onek_ref,
                 qm_ref, km_ref, vm_ref, qf_ref, kf_ref, vf_ref, carry_ref):
    si = pl.program_id(1)
    za = za_ref[...]
    cos = cos_ref[...]
    sin = sin_ref[...]
    g128, g128t = g128_ref[...], g128t_ref[...]
    g64, g64t = g64_ref[...], g64t_ref[...]

    cq = _rms_rows(za[:, 0:MLA_Q_LORA], gqa_ref[...]).astype(BF16)
    ckv = _rms_rows(za[:, MLA_Q_LORA:MLA_Q_LORA + MLA_KV_LORA], gkva_ref[...]).astype(BF16)
    last = za[:, ZA_COLS - LANES:ZA_COLS]
    q = _dot(cq, wuq_ref[...])
    kv = _dot(ckv, wukv_ref[...])
    n_nope = MLA_HEADS * MLA_NOPE
    n_rope = MLA_HEADS * MLA_ROPE
    qn = _group_rms(q[:, :n_nope], g128, g128t, MLA_NOPE, gqn_ref[...])
    qr = _group_rms(q[:, n_nope:], g64[:n_rope], g64t[:, :n_rope], MLA_ROPE, gqr_ref[...])
    cos4 = jnp.concatenate([cos] * (n_rope // LANES), axis=-1)
    sin4 = jnp.concatenate([sin] * (n_rope // LANES), axis=-1)
    qr = qr * cos4 + _rot_half(qr) * sin4
    kn = _group_rms(kv[:, :n_nope], g128, g128t, MLA_NOPE, gkn_ref[...])
    lane = lax.broadcasted_iota(I32, last.shape, 1)
    kr_ss = jnp.sum(jnp.where(lane < MLA_ROPE, last * last, 0.0), axis=-1, keepdims=True)
    kr = last * lax.rsqrt(kr_ss * (1.0 / MLA_ROPE) + EPS) * gkr_ref[...]
    kr = kr * cos + _rot_half(kr) * sin
    kr_b = kr[:, :MLA_ROPE].astype(BF16)
    tm = za.shape[0]
    ones_rows = jnp.where(lax.broadcasted_iota(I32, (V_PAD, tm), 0) == 0, 1.0, 0.0).astype(BF16)
    qn_t = qn.T.astype(BF16)
    qr_t = qr.T.astype(BF16)
    v_t = kv[:, n_nope:].T.astype(BF16)
    for h in range(MLA_HEADS):
        qm_ref[0, h, 0:MLA_NOPE, :] = qn_t[h * MLA_NOPE:(h + 1) * MLA_NOPE, :]
        qm_ref[0, h, MLA_NOPE:MLA_NOPE + MLA_ROPE, :] = qr_t[h * MLA_ROPE:(h + 1) * MLA_ROPE, :]
        km_ref[0, h, :, 0:MLA_NOPE] = kn[:, h * MLA_NOPE:(h + 1) * MLA_NOPE].astype(BF16)
        km_ref[0, h, :, MLA_NOPE:MLA_NOPE + MLA_ROPE] = kr_b
        vm_ref[0, h, 0:MLA_V, :] = v_t[h * MLA_V:(h + 1) * MLA_V, :]
        vm_ref[0, h, MLA_V:MLA_V + V_PAD, :] = ones_rows

    nf = FOX_HEADS * FOX_HD
    zf = zf_ref[...]
    fq = _group_rms(zf[:, 0:nf].astype(F32), g64, g64t, FOX_HD, gfq_ref[...])
    fk = _group_rms(zf[:, nf:2 * nf].astype(F32), g64, g64t, FOX_HD, gfk_ref[...])
    fv = zf[:, 2 * nf:3 * nf]

    @pl.when(si == 0)
    def _():
        carry_ref[...] = jnp.zeros_like(carry_ref)

    logf = _log_sigmoid(last + fb_ref[...])
    l_hi, l_mid, l_lo = _split3(logf)
    tri = tri_ref[...]
    cum = carry_ref[...] + (_dot(tri, l_hi) + _dot(tri, l_mid) + _dot(tri, l_lo))
    tm = cum.shape[0]
    carry_ref[...] = cum[tm - 1:tm, :]
    fcat = jnp.concatenate(_split3(cum * LOG2E), axis=-1)
    augq = _dot(fcat, eq_ref[...]) + oneq_ref[...]
    augk = _dot(fcat, ek_ref[...]) + onek_ref[...]
    fq_t = fq.T.astype(BF16)
    augq_t = augq.T.astype(BF16)
    fv_t = fv.astype(F32).T.astype(BF16)
    for h in range(FOX_HEADS):
        sl = slice(h * FOX_HD, (h + 1) * FOX_HD)
        qf_ref[0, h, 0:FOX_HD, :] = fq_t[sl, :]
        qf_ref[0, h, FOX_HD:2 * FOX_HD, :] = augq_t[sl, :]
        kf_ref[0, h] = jnp.concatenate([fk[:, sl], augk[:, sl]], axis=-1).astype(BF16)
        vf_ref[0, h, 0:FOX_HD, :] = fv_t[sl, :]
        vf_ref[0, h, FOX_HD:FOX_HD + V_PAD, :] = ones_rows


def _group_matrix(cols, group):
    m = np.zeros((cols, LANES), np.float32)
    m[np.arange(cols), np.arange(cols) // group] = 1.0
    return m


def _bias_column_matrices():
    nf = FOX_HEADS * FOX_HD
    eq = np.zeros((3 * LANES, nf), np.float32)
    ek = np.zeros((3 * LANES, nf), np.float32)
    oneq = np.zeros((1, nf), np.float32)
    onek = np.zeros((1, nf), np.float32)
    for h in range(FOX_HEADS):
        for t in range(3):
            eq[t * LANES + FLOGIT_LANE + h, h * FOX_HD + t] = 1.0
            ek[t * LANES + FLOGIT_LANE + h, h * FOX_HD + 3 + t] = -1.0
            oneq[0, h * FOX_HD + 3 + t] = 1.0
            onek[0, h * FOX_HD + t] = 1.0
    return eq, ek, oneq, onek


def _prep(za, zf, cos, sin, wuq, wukv, gains, batch, seq):
    tm = TM_PROJ
    nt = seq // tm
    g128 = _group_matrix(MLA_HEADS * MLA_NOPE, MLA_NOPE)
    g64 = _group_matrix(FOX_HEADS * FOX_HD, FOX_HD)
    tri = np.tril(np.ones((tm, tm), np.float32))
    eq, ek, oneq, onek = _bias_column_matrices()
    consts = [jnp.asarray(g128, BF16), jnp.asarray(g128.T, BF16), jnp.asarray(g64, BF16), jnp.asarray(g64.T, BF16),
              jnp.asarray(tri, BF16), jnp.asarray(eq, BF16), jnp.asarray(ek, BF16),
              jnp.asarray(oneq, F32), jnp.asarray(onek, F32)]
    row = lambda c: pl.BlockSpec((tm, c), lambda b, s: (b * nt + s, 0))
    tab = pl.BlockSpec((tm, LANES), lambda b, s: (s, 0))
    head = lambda h, d: pl.BlockSpec((1, h, tm, d), lambda b, s: (b, 0, s, 0))
    head_t = lambda h, d: pl.BlockSpec((1, h, d, tm), lambda b, s: (b, 0, 0, s))
    hshape = lambda h, d: jax.ShapeDtypeStruct((batch, h, seq, d), BF16)
    hshape_t = lambda h, d: jax.ShapeDtypeStruct((batch, h, d, seq), BF16)
    dk = MLA_NOPE + MLA_ROPE
    return pl.pallas_call(
        _prep_kernel,
        grid=(batch, nt),
        in_specs=[row(za.shape[1]), row(zf.shape[1]), tab, tab, _full(wuq.shape), _full(wukv.shape)]
                 + [_full(g.shape) for g in gains] + [_full(c.shape) for c in consts],
        out_specs=[head_t(MLA_HEADS, dk), head(MLA_HEADS, dk), head_t(MLA_HEADS, MLA_V + V_PAD),
                   head_t(FOX_HEADS, FOX_DK), head(FOX_HEADS, FOX_DK), head_t(FOX_HEADS, FOX_HD + V_PAD)],
        out_shape=[hshape_t(MLA_HEADS, dk), hshape(MLA_HEADS, dk), hshape_t(MLA_HEADS, MLA_V + V_PAD),
                   hshape_t(FOX_HEADS, FOX_DK), hshape(FOX_HEADS, FOX_DK), hshape_t(FOX_HEADS, FOX_HD + V_PAD)],
        scratch_shapes=[pltpu.VMEM((1, LANES), F32)],
        compiler_params=_cparams(("arbitrary", "arbitrary")),
        name="prep",
    )(za, zf, cos, sin, wuq, wukv, *gains, *consts)


def _attn_kernel(qt_ref, k_ref, vt_ref, o_ref, *scratch, tq, tk, hpb, dv):
    qi = pl.program_id(2)
    m_refs, acc_refs, s_refs = scratch[:hpb], scratch[hpb:2 * hpb], scratch[2 * hpb:]
    n_ahead = len(s_refs)

    def score(hh, j):
        return _dot(k_ref[0, hh, pl.ds(pl.multiple_of(j * tk, tk), tk), :], qt_ref[0, hh])

    def kv_block(j, diagonal):
        start = pl.multiple_of(j * tk, tk)
        state = [(m_refs[hh][...], acc_refs[hh][...]) for hh in range(hpb)]
        ready = {hh: s_refs[hh][...] for hh in range(n_ahead)}
        upcoming = list(range(n_ahead, hpb)) + ([] if diagonal else [("next", hh) for hh in range(n_ahead)])

        def issue_next_score():
            if not upcoming:
                return
            item = upcoming.pop(0)
            if isinstance(item, tuple):
                s_refs[item[1]][...] = score(item[1], j + 1)
            else:
                ready[item] = score(item, j)

        issue_next_score()
        results = []
        for hh in range(hpb):
            s, (m_old, acc_old) = ready.pop(hh), state[hh]
            if diagonal:
                key = lax.broadcasted_iota(I32, s.shape, 0)
                qry = lax.broadcasted_iota(I32, s.shape, 1)
                s = jnp.where(key <= qry, s, NEG_BIG)
            m_new = jnp.maximum(m_old, jnp.max(s, axis=0, keepdims=True))
            p = jnp.exp2((s - m_new).astype(BF16))
            pv = _dot(vt_ref[0, hh, :, pl.ds(start, tk)], p)
            results.append((m_new, acc_old * jnp.exp2(m_old - m_new) + pv))
            issue_next_score()
        while upcoming:
            issue_next_score()
        for hh in range(hpb):
            m_refs[hh][...] = results[hh][0]
            acc_refs[hh][...] = results[hh][1]

    for hh in range(hpb):
        m_refs[hh][...] = jnp.full(m_refs[hh].shape, NEG_BIG, F32)
        acc_refs[hh][...] = jnp.zeros(acc_refs[hh].shape, F32)
    for hh in range(n_ahead):
        s_refs[hh][...] = score(hh, 0)

    def full_block(j, carry):
        kv_block(j, False)
        return carry

    lax.fori_loop(0, qi, full_block, 0)
    kv_block(qi, True)
    outs = []
    for hh in range(hpb):
        acc = acc_refs[hh][...]
        outs.append(acc[:dv, :] / acc[dv:dv + 1, :])
    o_ref[0] = jnp.concatenate(outs, axis=0).T.astype(o_ref.dtype)


def _attention(qt, k, vt, hpb, dv):
    b, h, s, dk = k.shape
    dvp = vt.shape[2]
    tq = tk = ATTN_TQ
    assert s % tq == 0 and h % hpb == 0 and (hpb * dv) % LANES == 0 and dvp == dv + V_PAD
    kern = functools.partial(_attn_kernel, tq=tq, tk=tk, hpb=hpb, dv=dv)
    kv_bytes = hpb * s * (-(-dk // LANES) * LANES + dvp) * 2
    kv_mode = {} if 2 * kv_bytes <= VMEM_LIMIT // 2 else {"pipeline_mode": pl.Buffered(1)}
    return pl.pallas_call(
        kern,
        grid=(b, h // hpb, s // tq),
        in_specs=[pl.BlockSpec((1, hpb, dk, tq), lambda bi, hi, qi: (bi, hi, 0, qi)),
                  pl.BlockSpec((1, hpb, s, dk), lambda bi, hi, qi: (bi, hi, 0, 0), **kv_mode),
                  pl.BlockSpec((1, hpb, dvp, s), lambda bi, hi, qi: (bi, hi, 0, 0), **kv_mode)],
        out_specs=pl.BlockSpec((1, tq, hpb * dv), lambda bi, hi, qi: (bi, qi, hi)),
        out_shape=jax.ShapeDtypeStruct((b, s, h * dv), BF16),
        scratch_shapes=([pltpu.VMEM((1, tq), F32)] * hpb + [pltpu.VMEM((dvp, tq), F32)] * hpb
                        + [pltpu.VMEM((tk, tq), F32)] * hpb),
        compiler_params=_cparams(("arbitrary", "arbitrary", "arbitrary")),
        name="attention",
    )(qt, k, vt)


def _outproj_kernel(x_ref, om_ref, of_ref, zg_ref, wom_ref, wof_ref, wo_ref, gffn_ref, wr_ref, br_ref,
                    x1_ref, h2_ref, idx_ref, gate_ref):
    d = x_ref.shape[1]
    zg = zg_ref[...].astype(F32)
    mixed = zg[:, :d] * _dot(om_ref[...], wom_ref[...]) + zg[:, d:] * _dot(of_ref[...], wof_ref[...])
    x1 = x_ref[...] + _dot(mixed.astype(BF16), wo_ref[...])
    x1_ref[...] = x1
    h2 = _rms_rows(x1, gffn_ref[...])
    _store_token_tiles(h2_ref, h2)
    h_hi = h2.astype(BF16)
    h_lo = (h2 - h_hi.astype(F32)).astype(BF16)
    hi_terms = _dot(h_hi, wr_ref[...])
    logits = hi_terms[:, :LANES] + hi_terms[:, LANES:] + _dot(h_lo, wr_ref[:, :LANES]) + br_ref[...]
    lane = lax.broadcasted_iota(I32, logits.shape, 1)
    lane_f = lane.astype(F32)
    work = jnp.where(lane < N_EXPERTS, logits, -jnp.inf)
    vals, idxs = [], []
    for _ in range(TOP_K):
        m = jnp.max(work, axis=-1, keepdims=True)
        idx = jnp.min(jnp.where(work == m, lane_f, float(LANES)), axis=-1, keepdims=True)
        vals.append(m)
        idxs.append(idx)
        work = jnp.where(lane_f == idx, -jnp.inf, work)
    exps = [jnp.exp(v - vals[0]) for v in vals]
    denom = exps[0] + exps[1] + exps[2] + exps[3]
    idx_out = jnp.zeros(logits.shape, F32)
    gate_out = jnp.zeros(logits.shape, F32)
    for kk in range(TOP_K):
        idx_out = jnp.where(lane == kk, idxs[kk], idx_out)
        gate_out = jnp.where(lane == kk, exps[kk] / denom, gate_out)
    idx_ref[...] = idx_out.astype(I32)
    gate_ref[...] = gate_out


def _outproj(x2d, o_mla, o_fox, zg, wom, wof, wo, gffn, wr, br):
    n, d = x2d.shape
    tm = TM_PROJ
    row = lambda c: pl.BlockSpec((tm, c), lambda i: (i, 0))
    return pl.pallas_call(
        _outproj_kernel,
        grid=(n // tm,),
        in_specs=[row(d), row(o_mla.shape[1]), row(o_fox.shape[1]), row(zg.shape[1]),
                  _full(wom.shape), _full(wof.shape), _full(wo.shape), _full(gffn.shape),
                  _full(wr.shape), _full(br.shape)],
        out_specs=[row(d), pl.BlockSpec((tm * (d // LANES), LANES), lambda i: (i, 0)), row(LANES), row(LANES)],
        out_shape=[jax.ShapeDtypeStruct((n, d), F32), jax.ShapeDtypeStruct((n * (d // LANES), LANES), F32),
                   jax.ShapeDtypeStruct((n, LANES), I32), jax.ShapeDtypeStruct((n, LANES), F32)],
        compiler_params=_cparams(("arbitrary",)),
        name="outproj",
    )(x2d, o_mla, o_fox, zg, wom, wof, wo, gffn, wr, br)


def _route_kernel(idx_ref, lstrict_ref, ustrict_ref, dest_ref, be_ref, meta_ref, cnt_ref, start_ref, run_ref):
    phase = pl.program_id(0)
    i = pl.program_id(1)
    idx = idx_ref[...]
    lane = lax.broadcasted_iota(I32, idx.shape, 1)
    hit = [lane == idx[:, kk:kk + 1] for kk in range(TOP_K)]
    onehot = jnp.zeros(idx.shape, F32)
    for hk in hit:
        onehot = jnp.where(hk, 1.0, onehot)
    tile_cnt = jnp.sum(onehot, axis=0, keepdims=True)

    @pl.when((phase == 0) & (i == 0))
    def _():
        cnt_ref[...] = jnp.zeros_like(cnt_ref)

    @pl.when(phase == 0)
    def _():
        cnt_ref[...] += tile_cnt

    @pl.when((phase == 1) & (i == 0))
    def _():
        cnt = cnt_ref[...]
        nblk = jnp.floor((cnt + (MOE_ROWS - 1.0)) * (1.0 / MOE_ROWS))
        nblk8 = jnp.broadcast_to(nblk, (8, LANES)).astype(BF16)
        start = _dot(nblk8, ustrict_ref[...])[0:1, :]
        start_ref[...] = start
        run_ref[...] = jnp.zeros_like(run_ref)
        end = start + nblk
        nb = be_ref.shape[0]
        bid = lax.broadcasted_iota(I32, (nb, LANES), 0).astype(F32)
        lane_b = lax.broadcasted_iota(I32, (nb, LANES), 1)
        owned = jnp.where((lane_b < N_EXPERTS) & (end <= bid), 1.0, 0.0)
        be = jnp.minimum(jnp.sum(owned, axis=-1, keepdims=True), N_EXPERTS - 1.0)
        be_ref[...] = jnp.broadcast_to(be, (nb, LANES)).astype(I32)
        pad_lo = start * MOE_ROWS + cnt
        pad_hi = end * MOE_ROWS
        row8 = lax.broadcasted_iota(I32, (8, LANES), 0)
        meta = jnp.where(row8 == 0, pad_lo, jnp.where(row8 == 1, pad_hi, jnp.where(row8 == 2, end, 0.0)))
        meta_ref[...] = meta.astype(I32)

    @pl.when(phase == 1)
    def _():
        rank = _dot(lstrict_ref[...], onehot.astype(BF16))
        slot = start_ref[...] * MOE_ROWS + run_ref[...] + rank
        out = jnp.zeros(idx.shape, F32)
        for kk in range(TOP_K):
            dk = jnp.sum(jnp.where(hit[kk], slot, 0.0), axis=-1, keepdims=True)
            out = jnp.where(lane == kk, dk, out)
        dest_ref[...] = out.astype(I32)
        run_ref[...] += tile_cnt


def _route(idx, n_blocks):
    n = idx.shape[0]
    tm = TM_PROJ
    nt = n // tm
    nb = (n_blocks + 7) // 8 * 8
    lstrict = jnp.asarray(np.tril(np.ones((tm, tm), np.float32), -1), BF16)
    ustrict = jnp.asarray(np.triu(np.ones((LANES, LANES), np.float32), 1), BF16)
    return pl.pallas_call(
        _route_kernel,
        grid=(2, nt),
        in_specs=[pl.BlockSpec((tm, LANES), lambda p, i: (i, 0)), _full(lstrict.shape), _full(ustrict.shape)],
        out_specs=[pl.BlockSpec((tm, LANES), lambda p, i: (i * p, 0)), _full((nb, LANES)), _full((8, LANES))],
        out_shape=[jax.ShapeDtypeStruct((n, LANES), I32), jax.ShapeDtypeStruct((nb, LANES), I32),
                   jax.ShapeDtypeStruct((8, LANES), I32)],
        scratch_shapes=[pltpu.VMEM((1, LANES), F32), pltpu.VMEM((1, LANES), F32), pltpu.VMEM((1, LANES), F32)],
        compiler_params=_cparams(("arbitrary", "arbitrary")),
        name="route",
    )(idx, lstrict, ustrict)


def _dispatch_kernel(meta_ref, dest_ref, h_ref, xs_ref, zero_ref, sem, zsem, *, nch):
    i = pl.program_id(0)
    tm = h_ref.shape[0] // nch

    def tile(ref, r):
        return ref.at[pl.ds(pl.multiple_of(r * nch, nch), nch), :]

    def row_copy(t, kk):
        return pltpu.make_async_copy(tile(h_ref, t), tile(xs_ref, dest_ref[t * TOP_K + kk]), sem)

    def issue(t, carry):
        for kk in range(TOP_K):
            row_copy(t, kk).start(priority=kk % 2)
        return carry

    def drain(t, carry):
        for kk in range(TOP_K):
            row_copy(t, kk).wait()
        return carry

    lax.fori_loop(0, tm, issue, 0)

    @pl.when(i == pl.num_programs(0) - 1)
    def _():
        zero_ref[...] = jnp.zeros_like(zero_ref)
        blk = MOE_ROWS * nch

        def pad_copy(r):
            return pltpu.make_async_copy(tile(zero_ref, 0), tile(xs_ref, r), zsem)

        def blk_copy(b):
            return pltpu.make_async_copy(zero_ref, xs_ref.at[pl.ds(pl.multiple_of(b * blk, blk), blk), :], zsem)

        for e in range(N_EXPERTS):
            lo, hi = meta_ref[0, e], meta_ref[1, e]
            lax.fori_loop(lo, hi, lambda r, c: (pad_copy(r).start(), c)[1], 0)
            lax.fori_loop(lo, hi, lambda r, c: (pad_copy(r).wait(), c)[1], 0)
        n_used = meta_ref[2, N_EXPERTS - 1]
        n_blocks = xs_ref.shape[0] // blk
        lax.fori_loop(n_used, n_blocks, lambda b, c: (blk_copy(b).start(), c)[1], 0)
        lax.fori_loop(n_used, n_blocks, lambda b, c: (blk_copy(b).wait(), c)[1], 0)

    lax.fori_loop(0, tm, drain, 0)


def _dispatch(h2t, dest_flat, meta, n_rows, nch):
    n = h2t.shape[0] // nch
    tm = TM_DMA
    grid_spec = pltpu.PrefetchScalarGridSpec(
        num_scalar_prefetch=1,
        grid=(n // tm,),
        in_specs=[pl.BlockSpec((tm * TOP_K,), lambda i, meta: (i,), memory_space=pltpu.SMEM),
                  pl.BlockSpec((tm * nch, LANES), lambda i, meta: (i, 0))],
        out_specs=pl.BlockSpec(memory_space=pl.ANY),
        scratch_shapes=[pltpu.VMEM((MOE_ROWS * nch, LANES), F32), pltpu.SemaphoreType.DMA(()),
                        pltpu.SemaphoreType.DMA(())],
    )
    return pl.pallas_call(
        functools.partial(_dispatch_kernel, nch=nch),
        grid_spec=grid_spec,
        out_shape=jax.ShapeDtypeStruct((n_rows * nch, LANES), F32),
        compiler_params=_cparams(("arbitrary",)),
        name="dispatch",
    )(meta, dest_flat, h2t)


def _experts_kernel(be_ref, xs_ref, wu_ref, bu_ref, wd_ref, bd_ref, y_ref, wu_bf, wd_bf):
    b = pl.program_id(0)
    ff = wd_ref.shape[1]
    prev = be_ref[jnp.maximum(b - 1, 0)]

    @pl.when((b == 0) | (be_ref[b] != prev))
    def _():
        wu_bf[...] = wu_ref[0].astype(BF16)
        wd_bf[...] = wd_ref[0].astype(BF16)

    nch = wu_ref.shape[1] // LANES
    x = _load_token_tiles(xs_ref, xs_ref.shape[0] // nch, nch)
    gu = _dot(x.astype(BF16), wu_bf[...]) + bu_ref[0]
    g = jnp.minimum(gu[:, :ff], SWIGLU_LIMIT)
    u = jnp.clip(gu[:, ff:], -SWIGLU_LIMIT, SWIGLU_LIMIT)
    act = g * jax.nn.sigmoid(SWIGLU_ALPHA * g) * (u + 1.0)
    _store_token_tiles(y_ref, _dot(act.astype(BF16), wd_bf[...]) + bd_ref[0])


def _experts(xs, block_expert, w_up, b_up, w_down, b_down):
    e, d, ff2 = w_up.shape
    ff = w_down.shape[1]
    nch = d // LANES
    n_rows = xs.shape[0] // nch
    grid_spec = pltpu.PrefetchScalarGridSpec(
        num_scalar_prefetch=1,
        grid=(n_rows // MOE_ROWS,),
        in_specs=[pl.BlockSpec((MOE_ROWS * nch, LANES), lambda b, be: (b, 0)),
                  pl.BlockSpec((1, d, ff2), lambda b, be: (be[b], 0, 0)),
                  pl.BlockSpec((1, 1, ff2), lambda b, be: (be[b], 0, 0)),
                  pl.BlockSpec((1, ff, d), lambda b, be: (be[b], 0, 0)),
                  pl.BlockSpec((1, 1, d), lambda b, be: (be[b], 0, 0))],
        out_specs=pl.BlockSpec((MOE_ROWS * nch, LANES), lambda b, be: (b, 0)),
        scratch_shapes=[pltpu.VMEM((d, ff2), BF16), pltpu.VMEM((ff, d), BF16)],
    )
    return pl.pallas_call(
        _experts_kernel,
        grid_spec=grid_spec,
        out_shape=jax.ShapeDtypeStruct((n_rows * nch, LANES), F32),
        compiler_params=_cparams(("arbitrary",)),
        name="experts",
    )(block_expert, xs, w_up, b_up.reshape(e, 1, ff2), w_down, b_down.reshape(e, 1, d))


def _combine_kernel(dest_ref, dest_next_ref, x1_ref, gate_ref, p_ref, gple_ref, wpg_ref, wpp_ref, ys_ref, o_ref,
                    rows_ref, sem):
    i = pl.program_id(0)
    tm, d = x1_ref.shape
    nch = d // LANES
    slot = lax.rem(i, 2)

    def row_copy(idx_ref, buf, t, kk):
        src = pl.multiple_of(idx_ref[t * TOP_K + kk] * nch, nch)
        dst = pl.multiple_of(t * nch, nch)
        return pltpu.make_async_copy(ys_ref.at[pl.ds(src, nch), :], rows_ref.at[buf, kk, pl.ds(dst, nch), :],
                                     sem.at[buf])

    def gather(idx_ref, buf):
        def issue(t, carry):
            for kk in range(TOP_K):
                row_copy(idx_ref, buf, t, kk).start(priority=kk % 2)
            return carry
        lax.fori_loop(0, tm, issue, 0)

    @pl.when(i == 0)
    def _():
        gather(dest_ref, 0)

    @pl.when(i + 1 < pl.num_programs(0))
    def _():
        gather(dest_next_ref, 1 - slot)

    proj = _dot(p_ref[...].astype(BF16), wpp_ref[...])

    def drain(t, carry):
        for kk in range(TOP_K):
            row_copy(dest_ref, slot, t, kk).wait()
        return carry

    lax.fori_loop(0, tm, drain, 0)
    gates = gate_ref[...]
    moe = gates[:, 0:1] * _load_token_tiles(rows_ref.at[slot, 0], tm, nch)
    for kk in range(1, TOP_K):
        moe = moe + gates[:, kk:kk + 1] * _load_token_tiles(rows_ref.at[slot, kk], tm, nch)
    x2 = x1_ref[...] + moe
    hn = _rms_rows(x2, gple_ref[...]).astype(BF16)
    o_ref[...] = x2 + jax.nn.sigmoid(_dot(hn, wpg_ref[...])) * proj


def _combine(x1, gates, dest_flat, p2d, gple, wpg, wpp, ys):
    n, d = x1.shape
    tm = TM_DMA
    nt = n // tm
    row = lambda c: pl.BlockSpec((tm, c), lambda i: (i, 0))
    return pl.pallas_call(
        _combine_kernel,
        grid=(nt,),
        in_specs=[pl.BlockSpec((tm * TOP_K,), lambda i: (i,), memory_space=pltpu.SMEM),
                  pl.BlockSpec((tm * TOP_K,), lambda i: (jnp.minimum(i + 1, nt - 1),), memory_space=pltpu.SMEM),
                  row(d), row(LANES), row(p2d.shape[1]), _full(gple.shape), _full(wpg.shape), _full(wpp.shape),
                  pl.BlockSpec(memory_space=pl.ANY)],
        out_specs=row(d),
        out_shape=jax.ShapeDtypeStruct((n, d), F32),
        scratch_shapes=[pltpu.VMEM((2, TOP_K, tm * (d // LANES), LANES), F32), pltpu.SemaphoreType.DMA((2,))],
        compiler_params=_cparams(("arbitrary",)),
        name="combine",
    )(dest_flat, dest_flat, x1, gates, p2d, gple, wpg, wpp, ys)


def _rope_tables(seq):
    pos = jnp.arange(seq, dtype=F32)
    inv_freq = ROPE_THETA ** (-jnp.arange(0, MLA_ROPE, 2, dtype=F32) / MLA_ROPE)
    ang = pos[:, None] * inv_freq[None, :]
    c, s = jnp.cos(ang), jnp.sin(ang)
    reps = LANES // MLA_ROPE
    return jnp.tile(jnp.concatenate([c, c], -1), (1, reps)), jnp.tile(jnp.concatenate([-s, s], -1), (1, reps))


def _layer(x2d, p2d, batch, seq, cos, sin, attn_norm, w_in, q_a_norm, w_uq, kv_a_norm, w_ukv,
           mla_q_nope_norm, mla_q_rope_norm, mla_k_nope_norm, mla_k_rope_norm,
           fox_q_norm, fox_k_norm, fox_f_bias, w_o_mla, w_o_fox, w_o,
           ffn_norm, w_router, b_router, w_up, b_up, w_down, b_down, ple_norm, w_ple_gate, w_ple_proj):
    n, d = x2d.shape
    nf = FOX_HEADS * FOX_HD
    row1 = lambda v: v.reshape(1, -1).astype(F32)

    o = np.cumsum([0, MLA_Q_LORA, MLA_KV_LORA, MLA_ROPE, nf, nf, nf, FOX_HEADS, d, d])
    wa = jnp.concatenate([w_in[:, o[0]:o[3]], w_in[:, o[6]:o[7]],
                          jnp.zeros((d, ZA_COLS - int(o[3]) - FOX_HEADS), w_in.dtype)], axis=1).astype(BF16)
    wf = w_in[:, o[3]:o[6]].astype(BF16)
    wg = w_in[:, o[7]:o[9]].astype(BF16)
    za, zf, zg = _inproj(x2d, row1(attn_norm), wa, wf, wg)

    wq3 = w_uq.reshape(MLA_Q_LORA, MLA_HEADS, MLA_NOPE + MLA_ROPE)
    wuq = jnp.concatenate([wq3[:, :, :MLA_NOPE].reshape(MLA_Q_LORA, -1),
                           wq3[:, :, MLA_NOPE:].reshape(MLA_Q_LORA, -1)], axis=1).astype(BF16)
    wkv3 = w_ukv.reshape(MLA_KV_LORA, MLA_HEADS, MLA_NOPE + MLA_V)
    wukv = jnp.concatenate([wkv3[:, :, :MLA_NOPE].reshape(MLA_KV_LORA, -1),
                            wkv3[:, :, MLA_NOPE:].reshape(MLA_KV_LORA, -1)], axis=1).astype(BF16)
    mla_scale = (MLA_NOPE + MLA_ROPE) ** -0.5 * LOG2E
    fox_scale = FOX_HD ** -0.5 * LOG2E
    gkr = jnp.concatenate([mla_k_rope_norm.astype(F32), jnp.zeros((LANES - MLA_ROPE,), F32)])
    fb = jnp.zeros((LANES,), F32).at[FLOGIT_LANE:FLOGIT_LANE + FOX_HEADS].set(fox_f_bias.astype(F32))
    gains = [row1(q_a_norm), row1(kv_a_norm),
             row1(jnp.tile(mla_q_nope_norm, MLA_HEADS) * mla_scale),
             row1(jnp.tile(mla_q_rope_norm, MLA_HEADS) * mla_scale),
             row1(jnp.tile(mla_k_nope_norm, MLA_HEADS)), row1(gkr),
             row1(jnp.tile(fox_q_norm, FOX_HEADS) * fox_scale), row1(jnp.tile(fox_k_norm, FOX_HEADS)), row1(fb)]
    qm, km, vm, qf, kf, vf = _prep(za, zf, cos, sin, wuq, wukv, gains, batch, seq)

    o_mla = _attention(qm, km, vm, hpb=4, dv=MLA_V).reshape(n, MLA_HEADS * MLA_V)
    o_fox = _attention(qf, kf, vf, hpb=4, dv=FOX_HD).reshape(n, nf)

    wr = jnp.concatenate([w_router.astype(F32), jnp.zeros((d, LANES - N_EXPERTS), F32)], axis=1)
    wr_hi = wr.astype(BF16)
    wr = jnp.concatenate([wr_hi, (wr - wr_hi.astype(F32)).astype(BF16)], axis=1)
    br = jnp.concatenate([b_router.astype(F32), jnp.zeros((LANES - N_EXPERTS,), F32)]).reshape(1, LANES)
    x1, h2, idx, gates = _outproj(x2d, o_mla, o_fox, zg, w_o_mla.astype(BF16), w_o_fox.astype(BF16),
                                  w_o.astype(BF16), row1(ffn_norm), wr, br)

    n_blocks = n * TOP_K // MOE_ROWS + N_EXPERTS
    dest, be, meta = _route(idx, n_blocks)
    dest_flat = dest[:, :TOP_K].reshape(-1)
    xs = _dispatch(h2, dest_flat, meta, n_blocks * MOE_ROWS, d // LANES)
    ys = _experts(xs, be[:n_blocks, 0], w_up, b_up, w_down, b_down)
    return _combine(x1, gates, dest_flat, p2d, row1(ple_norm), w_ple_gate.astype(BF16),
                    w_ple_proj.astype(BF16), ys)


def kernel(x, p, attn_norm, w_in, q_a_norm, w_uq, kv_a_norm, w_ukv, mla_q_nope_norm, mla_q_rope_norm,
           mla_k_nope_norm, mla_k_rope_norm, fox_q_norm, fox_k_norm, fox_f_bias, w_o_mla, w_o_fox, w_o,
           ffn_norm, w_router, b_router, w_up, b_up, w_down, b_down, ple_norm, w_ple_gate, w_ple_proj):
    batch, seq, d = x.shape
    depth = p.shape[0]
    cos, sin = _rope_tables(seq)
    x2d = x.reshape(batch * seq, d)
    for i in range(depth):
        x2d = _layer(x2d, p[i].reshape(batch * seq, -1), batch, seq, cos, sin,
                     attn_norm[i], w_in[i], q_a_norm[i], w_uq[i], kv_a_norm[i], w_ukv[i],
                     mla_q_nope_norm[i], mla_q_rope_norm[i], mla_k_nope_norm[i], mla_k_rope_norm[i],
                     fox_q_norm[i], fox_k_norm[i], fox_f_bias[i], w_o_mla[i], w_o_fox[i], w_o[i],
                     ffn_norm[i], w_router[i], b_router[i], w_up[i], b_up[i], w_down[i], b_down[i],
                     ple_norm[i], w_ple_gate[i], w_ple_proj[i])
    return x2d.reshape(batch, seq, d)
```

```python
import functools

import numpy as np
import jax
import jax.numpy as jnp
from jax import lax
from jax.experimental import pallas as pl
from jax.experimental.pallas import tpu as pltpu

F32 = jnp.float32
BF16 = jnp.bfloat16
I32 = jnp.int32

EPS = 1e-6
ROPE_THETA = 10000.0
MLA_HEADS = 8
MLA_NOPE = 128
MLA_ROPE = 64
MLA_V = 128
MLA_Q_LORA = 256
MLA_KV_LORA = 256
FOX_HEADS = 16
FOX_HD = 64
N_EXPERTS = 32
TOP_K = 4
SWIGLU_LIMIT = 7.0
SWIGLU_ALPHA = 1.702

LANES = 128
ZA_COLS = 640
FLOGIT_LANE = 64
FOX_DK = 128
V_PAD = 16
MOE_ROWS = 256
NEG_BIG = -1e30
LOG2E = 1.4426950408889634
VMEM_LIMIT = 56 * 1024 * 1024

TM_PROJ = 512
TM_DMA = 256
ATTN_TQ = 512


def _cparams(sem):
    return pltpu.CompilerParams(dimension_semantics=sem, vmem_limit_bytes=VMEM_LIMIT)


def _full(shape):
    nd = len(shape)
    return pl.BlockSpec(shape, lambda *_: (0,) * nd)


def _rms_rows(x, gain):
    return x * lax.rsqrt(jnp.mean(x * x, axis=-1, keepdims=True) + EPS) * gain


def _split3(x):
    hi = x.astype(BF16)
    r1 = x - hi.astype(F32)
    mid = r1.astype(BF16)
    lo = (r1 - mid.astype(F32)).astype(BF16)
    return hi, mid, lo


def _dot(a, b):
    return jnp.dot(a, b, preferred_element_type=F32)


def _store_token_tiles(ref, x):
    nch = x.shape[1] // LANES
    for c in range(nch):
        ref[pl.ds(c, x.shape[0], stride=nch), :] = x[:, c * LANES:(c + 1) * LANES]


def _load_token_tiles(ref, rows, nch):
    return jnp.concatenate([ref[pl.ds(c, rows, stride=nch), :] for c in range(nch)], axis=-1)


def _inproj_kernel(x_ref, g_ref, wa_ref, wf_ref, wg_ref, za_ref, zf_ref, zg_ref):
    hb = _rms_rows(x_ref[...], g_ref[...]).astype(BF16)
    za_ref[...] = _dot(hb, wa_ref[...])
    zf_ref[...] = _dot(hb, wf_ref[...]).astype(BF16)
    zg_ref[...] = jax.nn.sigmoid(_dot(hb, wg_ref[...])).astype(BF16)


def _inproj(x2d, gain, wa, wf, wg):
    n, d = x2d.shape
    tm = TM_PROJ
    row = lambda c: pl.BlockSpec((tm, c), lambda i: (i, 0))
    return pl.pallas_call(
        _inproj_kernel,
        grid=(n // tm,),
        in_specs=[row(d), _full(gain.shape), _full(wa.shape), _full(wf.shape), _full(wg.shape)],
        out_specs=[row(wa.shape[1]), row(wf.shape[1]), row(wg.shape[1])],
        out_shape=[jax.ShapeDtypeStruct((n, wa.shape[1]), F32),
                   jax.ShapeDtypeStruct((n, wf.shape[1]), BF16),
                   jax.ShapeDtypeStruct((n, wg.shape[1]), BF16)],
        compiler_params=_cparams(("arbitrary",)),
        name="inproj",
    )(x2d, gain, wa, wf, wg)


def _group_rms(x, gmat, gmat_t, group, gain):
    ss = _dot((x * x).astype(BF16), gmat)
    inv = lax.rsqrt(ss * (1.0 / group) + EPS)
    inv_hi = inv.astype(BF16)
    inv_lo = (inv - inv_hi.astype(F32)).astype(BF16)
    inv_full = _dot(inv_hi, gmat_t) + _dot(inv_lo, gmat_t)
    return x * inv_full * gain


def _rot_half(x):
    c = x.shape[-1]
    lane = lax.broadcasted_iota(I32, x.shape, 1)
    first = (lane & 63) < 32
    return jnp.where(first, pltpu.roll(x, c - 32, 1), pltpu.roll(x, 32, 1))


def _log_sigmoid(x):
    return jnp.minimum(x, 0.0) - jnp.log1p(jnp.exp(-jnp.abs(x)))


def _prep_kernel(za_ref, zf_ref, cos_ref, sin_ref, wuq_ref, wukv_ref,
                 gqa_ref, gkva_ref, gqn_ref, gqr_ref, gkn_ref, gkr_ref, gfq_ref, gfk_ref, fb_ref,
                 g128_ref, g128t_ref, g64_ref, g64t_ref, tri_ref, eq_ref, ek_ref, oneq_ref, onek_ref,
                 qm_ref, km_ref, vm_ref, qf_ref, kf_ref, vf_ref, carry_ref):
    si = pl.program_id(1)
    za = za_ref[...]
    cos = cos_ref[...]
    sin = sin_ref[...]
    g128, g128t = g128_ref[...], g128t_ref[...]
    g64, g64t = g64_ref[...], g64t_ref[...]

    cq = _rms_rows(za[:, 0:MLA_Q_LORA], gqa_ref[...]).astype(BF16)
    ckv = _rms_rows(za[:, MLA_Q_LORA:MLA_Q_LORA + MLA_KV_LORA], gkva_ref[...]).astype(BF16)
    last = za[:, ZA_COLS - LANES:ZA_COLS]
    q = _dot(cq, wuq_ref[...])
    kv = _dot(ckv, wukv_ref[...])
    n_nope = MLA_HEADS * MLA_NOPE
    n_rope = MLA_HEADS * MLA_ROPE
    qn = _group_rms(q[:, :n_nope], g128, g128t, MLA_NOPE, gqn_ref[...])
    qr = _group_rms(q[:, n_nope:], g64[:n_rope], g64t[:, :n_rope], MLA_ROPE, gqr_ref[...])
    cos4 = jnp.concatenate([cos] * (n_rope // LANES), axis=-1)
    sin4 = jnp.concatenate([sin] * (n_rope // LANES), axis=-1)
    qr = qr * cos4 + _rot_half(qr) * sin4
    kn = _group_rms(kv[:, :n_nope], g128, g128t, MLA_NOPE, gkn_ref[...])
    lane = lax.broadcasted_iota(I32, last.shape, 1)
    kr_ss = jnp.sum(jnp.where(lane < MLA_ROPE, last * last, 0.0), axis=-1, keepdims=True)
    kr = last * lax.rsqrt(kr_ss * (1.0 / MLA_ROPE) + EPS) * gkr_ref[...]
    kr = kr * cos + _rot_half(kr) * sin
    kr_b = kr[:, :MLA_ROPE].astype(BF16)
    tm = za.shape[0]
    ones_rows = jnp.where(lax.broadcasted_iota(I32, (V_PAD, tm), 0) == 0, 1.0, 0.0).astype(BF16)
    qn_t = qn.T.astype(BF16)
    qr_t = qr.T.astype(BF16)
    v_t = kv[:, n_nope:].T.astype(BF16)
    for h in range(MLA_HEADS):
        qm_ref[0, h, 0:MLA_NOPE, :] = qn_t[h * MLA_NOPE:(h + 1) * MLA_NOPE, :]
        qm_ref[0, h, MLA_NOPE:MLA_NOPE + MLA_ROPE, :] = qr_t[h * MLA_ROPE:(h + 1) * MLA_ROPE, :]
        km_ref[0, h, :, 0:MLA_NOPE] = kn[:, h * MLA_NOPE:(h + 1) * MLA_NOPE].astype(BF16)
        km_ref[0, h, :, MLA_NOPE:MLA_NOPE + MLA_ROPE] = kr_b
        vm_ref[0, h, 0:MLA_V, :] = v_t[h * MLA_V:(h + 1) * MLA_V, :]
        vm_ref[0, h, MLA_V:MLA_V + V_PAD, :] = ones_rows

    nf = FOX_HEADS * FOX_HD
    zf = zf_ref[...]
    fq = _group_rms(zf[:, 0:nf].astype(F32), g64, g64t, FOX_HD, gfq_ref[...])
    fk = _group_rms(zf[:, nf:2 * nf].astype(F32), g64, g64t, FOX_HD, gfk_ref[...])
    fv = zf[:, 2 * nf:3 * nf]

    @pl.when(si == 0)
    def _():
        carry_ref[...] = jnp.zeros_like(carry_ref)

    logf = _log_sigmoid(last + fb_ref[...])
    l_hi, l_mid, l_lo = _split3(logf)
    tri = tri_ref[...]
    cum = carry_ref[...] + (_dot(tri, l_hi) + _dot(tri, l_mid) + _dot(tri, l_lo))
    tm = cum.shape[0]
    carry_ref[...] = cum[tm - 1:tm, :]
    fcat = jnp.concatenate(_split3(cum * LOG2E), axis=-1)
    augq = _dot(fcat, eq_ref[...]) + oneq_ref[...]
    augk = _dot(fcat, ek_ref[...]) + onek_ref[...]
    fq_t = fq.T.astype(BF16)
    augq_t = augq.T.astype(BF16)
    fv_t = fv.astype(F32).T.astype(BF16)
    for h in range(FOX_HEADS):
        sl = slice(h * FOX_HD, (h + 1) * FOX_HD)
        qf_ref[0, h, 0:FOX_HD, :] = fq_t[sl, :]
        qf_ref[0, h, FOX_HD:2 * FOX_HD, :] = augq_t[sl, :]
        kf_ref[0, h] = jnp.concatenate([fk[:, sl], augk[:, sl]], axis=-1).astype(BF16)
        vf_ref[0, h, 0:FOX_HD, :] = fv_t[sl, :]
        vf_ref[0, h, FOX_HD:FOX_HD + V_PAD, :] = ones_rows


def _group_matrix(cols, group):
    m = np.zeros((cols, LANES), np.float32)
    m[np.arange(cols), np.arange(cols) // group] = 1.0
    return m


def _bias_column_matrices():
    nf = FOX_HEADS * FOX_HD
    eq = np.zeros((3 * LANES, nf), np.float32)
    ek = np.zeros((3 * LANES, nf), np.float32)
    oneq = np.zeros((1, nf), np.float32)
    onek = np.zeros((1, nf), np.float32)
    for h in range(FOX_HEADS):
        for t in range(3):
            eq[t * LANES + FLOGIT_LANE + h, h * FOX_HD + t] = 1.0
            ek[t * LANES + FLOGIT_LANE + h, h * FOX_HD + 3 + t] = -1.0
            oneq[0, h * FOX_HD + 3 + t] = 1.0
            onek[0, h * FOX_HD + t] = 1.0
    return eq, ek, oneq, onek


def _prep(za, zf, cos, sin, wuq, wukv, gains, batch, seq):
    tm = TM_PROJ
    nt = seq // tm
    g128 = _group_matrix(MLA_HEADS * MLA_NOPE, MLA_NOPE)
    g64 = _group_matrix(FOX_HEADS * FOX_HD, FOX_HD)
    tri = np.tril(np.ones((tm, tm), np.float32))
    eq, ek, oneq, onek = _bias_column_matrices()
    consts = [jnp.asarray(g128, BF16), jnp.asarray(g128.T, BF16), jnp.asarray(g64, BF16), jnp.asarray(g64.T, BF16),
              jnp.asarray(tri, BF16), jnp.asarray(eq, BF16), jnp.asarray(ek, BF16),
              jnp.asarray(oneq, F32), jnp.asarray(onek, F32)]
    row = lambda c: pl.BlockSpec((tm, c), lambda b, s: (b * nt + s, 0))
    tab = pl.BlockSpec((tm, LANES), lambda b, s: (s, 0))
    head = lambda h, d: pl.BlockSpec((1, h, tm, d), lambda b, s: (b, 0, s, 0))
    head_t = lambda h, d: pl.BlockSpec((1, h, d, tm), lambda b, s: (b, 0, 0, s))
    hshape = lambda h, d: jax.ShapeDtypeStruct((batch, h, seq, d), BF16)
    hshape_t = lambda h, d: jax.ShapeDtypeStruct((batch, h, d, seq), BF16)
    dk = MLA_NOPE + MLA_ROPE
    return pl.pallas_call(
        _prep_kernel,
        grid=(batch, nt),
        in_specs=[row(za.shape[1]), row(zf.shape[1]), tab, tab, _full(wuq.shape), _full(wukv.shape)]
                 + [_full(g.shape) for g in gains] + [_full(c.shape) for c in consts],
        out_specs=[head_t(MLA_HEADS, dk), head(MLA_HEADS, dk), head_t(MLA_HEADS, MLA_V + V_PAD),
                   head_t(FOX_HEADS, FOX_DK), head(FOX_HEADS, FOX_DK), head_t(FOX_HEADS, FOX_HD + V_PAD)],
        out_shape=[hshape_t(MLA_HEADS, dk), hshape(MLA_HEADS, dk), hshape_t(MLA_HEADS, MLA_V + V_PAD),
                   hshape_t(FOX_HEADS, FOX_DK), hshape(FOX_HEADS, FOX_DK), hshape_t(FOX_HEADS, FOX_HD + V_PAD)],
        scratch_shapes=[pltpu.VMEM((1, LANES), F32)],
        compiler_params=_cparams(("arbitrary", "arbitrary")),
        name="prep",
    )(za, zf, cos, sin, wuq, wukv, *gains, *consts)


def _attn_kernel(qt_ref, k_ref, vt_ref, o_ref, *scratch, tq, tk, hpb, dv):
    qi = pl.program_id(2)
    m_refs, acc_refs, s_refs = scratch[:hpb], scratch[hpb:2 * hpb], scratch[2 * hpb:]
    n_ahead = len(s_refs)

    def score(hh, j):
        return _dot(k_ref[0, hh, pl.ds(pl.multiple_of(j * tk, tk), tk), :], qt_ref[0, hh])

    def kv_block(j, diagonal):
        start = pl.multiple_of(j * tk, tk)
        state = [(m_refs[hh][...], acc_refs[hh][...]) for hh in range(hpb)]
        ready = {hh: s_refs[hh][...] for hh in range(n_ahead)}
        upcoming = list(range(n_ahead, hpb)) + ([] if diagonal else [("next", hh) for hh in range(n_ahead)])

        def issue_next_score():
            if not upcoming:
                return
            item = upcoming.pop(0)
            if isinstance(item, tuple):
                s_refs[item[1]][...] = score(item[1], j + 1)
            else:
                ready[item] = score(item, j)

        issue_next_score()
        results = []
        for hh in range(hpb):
            s, (m_old, acc_old) = ready.pop(hh), state[hh]
            if diagonal:
                key = lax.broadcasted_iota(I32, s.shape, 0)
                qry = lax.broadcasted_iota(I32, s.shape, 1)
                s = jnp.where(key <= qry, s, NEG_BIG)
            m_new = jnp.maximum(m_old, jnp.max(s, axis=0, keepdims=True))
            p = jnp.exp2((s - m_new).astype(BF16))
            pv = _dot(vt_ref[0, hh, :, pl.ds(start, tk)], p)
            results.append((m_new, acc_old * jnp.exp2(m_old - m_new) + pv))
            issue_next_score()
        while upcoming:
            issue_next_score()
        for hh in range(hpb):
            m_refs[hh][...] = results[hh][0]
            acc_refs[hh][...] = results[hh][1]

    for hh in range(hpb):
        m_refs[hh][...] = jnp.full(m_refs[hh].shape, NEG_BIG, F32)
        acc_refs[hh][...] = jnp.zeros(acc_refs[hh].shape, F32)
    for hh in range(n_ahead):
        s_refs[hh][...] = score(hh, 0)

    def full_block(j, carry):
        kv_block(j, False)
        return carry

    lax.fori_loop(0, qi, full_block, 0)
    kv_block(qi, True)
    outs = []
    for hh in range(hpb):
        acc = acc_refs[hh][...]
        outs.append(acc[:dv, :] / acc[dv:dv + 1, :])
    o_ref[0] = jnp.concatenate(outs, axis=0).T.astype(o_ref.dtype)


def _attention(qt, k, vt, hpb, dv):
    b, h, s, dk = k.shape
    dvp = vt.shape[2]
    tq = tk = ATTN_TQ
    assert s % tq == 0 and h % hpb == 0 and (hpb * dv) % LANES == 0 and dvp == dv + V_PAD
    kern = functools.partial(_attn_kernel, tq=tq, tk=tk, hpb=hpb, dv=dv)
    kv_bytes = hpb * s * (-(-dk // LANES) * LANES + dvp) * 2
    kv_mode = {} if 2 * kv_bytes <= VMEM_LIMIT // 2 else {"pipeline_mode": pl.Buffered(1)}
    return pl.pallas_call(
        kern,
        grid=(b, h // hpb, s // tq),
        in_specs=[pl.BlockSpec((1, hpb, dk, tq), lambda bi, hi, qi: (bi, hi, 0, qi)),
                  pl.BlockSpec((1, hpb, s, dk), lambda bi, hi, qi: (bi, hi, 0, 0), **kv_mode),
                  pl.BlockSpec((1, hpb, dvp, s), lambda bi, hi, qi: (bi, hi, 0, 0), **kv_mode)],
        out_specs=pl.BlockSpec((1, tq, hpb * dv), lambda bi, hi, qi: (bi, qi, hi)),
        out_shape=jax.ShapeDtypeStruct((b, s, h * dv), BF16),
        scratch_shapes=([pltpu.VMEM((1, tq), F32)] * hpb + [pltpu.VMEM((dvp, tq), F32)] * hpb
                        + [pltpu.VMEM((tk, tq), F32)] * hpb),
        compiler_params=_cparams(("arbitrary", "arbitrary", "arbitrary")),
        name="attention",
    )(qt, k, vt)


def _outproj_kernel(x_ref, om_ref, of_ref, zg_ref, wom_ref, wof_ref, wo_ref, gffn_ref, wr_ref, br_ref,
                    x1_ref, h2_ref, idx_ref, gate_ref):
    d = x_ref.shape[1]
    zg = zg_ref[...].astype(F32)
    mixed = zg[:, :d] * _dot(om_ref[...], wom_ref[...]) + zg[:, d:] * _dot(of_ref[...], wof_ref[...])
    x1 = x_ref[...] + _dot(mixed.astype(BF16), wo_ref[...])
    x1_ref[...] = x1
    h2 = _rms_rows(x1, gffn_ref[...])
    _store_token_tiles(h2_ref, h2)
    h_hi = h2.astype(BF16)
    h_lo = (h2 - h_hi.astype(F32)).astype(BF16)
    hi_terms = _dot(h_hi, wr_ref[...])
    logits = hi_terms[:, :LANES] + hi_terms[:, LANES:] + _dot(h_lo, wr_ref[:, :LANES]) + br_ref[...]
    lane = lax.broadcasted_iota(I32, logits.shape, 1)
    lane_f = lane.astype(F32)
    work = jnp.where(lane < N_EXPERTS, logits, -jnp.inf)
    vals, idxs = [], []
    for _ in range(TOP_K):
        m = jnp.max(work, axis=-1, keepdims=True)
        idx = jnp.min(jnp.where(work == m, lane_f, float(LANES)), axis=-1, keepdims=True)
        vals.append(m)
        idxs.append(idx)
        work = jnp.where(lane_f == idx, -jnp.inf, work)
    exps = [jnp.exp(v - vals[0]) for v in vals]
    denom = exps[0] + exps[1] + exps[2] + exps[3]
    idx_out = jnp.zeros(logits.shape, F32)
    gate_out = jnp.zeros(logits.shape, F32)
    for kk in range(TOP_K):
        idx_out = jnp.where(lane == kk, idxs[kk], idx_out)
        gate_out = jnp.where(lane == kk, exps[kk] / denom, gate_out)
    idx_ref[...] = idx_out.astype(I32)
    gate_ref[...] = gate_out


def _outproj(x2d, o_mla, o_fox, zg, wom, wof, wo, gffn, wr, br):
    n, d = x2d.shape
    tm = TM_PROJ
    row = lambda c: pl.BlockSpec((tm, c), lambda i: (i, 0))
    return pl.pallas_call(
        _outproj_kernel,
        grid=(n // tm,),
        in_specs=[row(d), row(o_mla.shape[1]), row(o_fox.shape[1]), row(zg.shape[1]),
                  _full(wom.shape), _full(wof.shape), _full(wo.shape), _full(gffn.shape),
                  _full(wr.shape), _full(br.shape)],
        out_specs=[row(d), pl.BlockSpec((tm * (d // LANES), LANES), lambda i: (i, 0)), row(LANES), row(LANES)],
        out_shape=[jax.ShapeDtypeStruct((n, d), F32), jax.ShapeDtypeStruct((n * (d // LANES), LANES), F32),
                   jax.ShapeDtypeStruct((n, LANES), I32), jax.ShapeDtypeStruct((n, LANES), F32)],
        compiler_params=_cparams(("arbitrary",)),
        name="outproj",
    )(x2d, o_mla, o_fox, zg, wom, wof, wo, gffn, wr, br)


def _route_kernel(idx_ref, lstrict_ref, ustrict_ref, dest_ref, be_ref, meta_ref, cnt_ref, start_ref, run_ref):
    phase = pl.program_id(0)
    i = pl.program_id(1)
    idx = idx_ref[...]
    lane = lax.broadcasted_iota(I32, idx.shape, 1)
    hit = [lane == idx[:, kk:kk + 1] for kk in range(TOP_K)]
    onehot = jnp.zeros(idx.shape, F32)
    for hk in hit:
        onehot = jnp.where(hk, 1.0, onehot)
    tile_cnt = jnp.sum(onehot, axis=0, keepdims=True)

    @pl.when((phase == 0) & (i == 0))
    def _():
        cnt_ref[...] = jnp.zeros_like(cnt_ref)

    @pl.when(phase == 0)
    def _():
        cnt_ref[...] += tile_cnt

    @pl.when((phase == 1) & (i == 0))
    def _():
        cnt = cnt_ref[...]
        nblk = jnp.floor((cnt + (MOE_ROWS - 1.0)) * (1.0 / MOE_ROWS))
        nblk8 = jnp.broadcast_to(nblk, (8, LANES)).astype(BF16)
        start = _dot(nblk8, ustrict_ref[...])[0:1, :]
        start_ref[...] = start
        run_ref[...] = jnp.zeros_like(run_ref)
        end = start + nblk
        nb = be_ref.shape[0]
        bid = lax.broadcasted_iota(I32, (nb, LANES), 0).astype(F32)
        lane_b = lax.broadcasted_iota(I32, (nb, LANES), 1)
        owned = jnp.where((lane_b < N_EXPERTS) & (end <= bid), 1.0, 0.0)
        be = jnp.minimum(jnp.sum(owned, axis=-1, keepdims=True), N_EXPERTS - 1.0)
        be_ref[...] = jnp.broadcast_to(be, (nb, LANES)).astype(I32)
        pad_lo = start * MOE_ROWS + cnt
        pad_hi = end * MOE_ROWS
        row8 = lax.broadcasted_iota(I32, (8, LANES), 0)
        meta = jnp.where(row8 == 0, pad_lo, jnp.where(row8 == 1, pad_hi, jnp.where(row8 == 2, end, 0.0)))
        meta_ref[...] = meta.astype(I32)

    @pl.when(phase == 1)
    def _():
        rank = _dot(lstrict_ref[...], onehot.astype(BF16))
        slot = start_ref[...] * MOE_ROWS + run_ref[...] + rank
        out = jnp.zeros(idx.shape, F32)
        for kk in range(TOP_K):
            dk = jnp.sum(jnp.where(hit[kk], slot, 0.0), axis=-1, keepdims=True)
            out = jnp.where(lane == kk, dk, out)
        dest_ref[...] = out.astype(I32)
        run_ref[...] += tile_cnt


def _route(idx, n_blocks):
    n = idx.shape[0]
    tm = TM_PROJ
    nt = n // tm
    nb = (n_blocks + 7) // 8 * 8
    lstrict = jnp.asarray(np.tril(np.ones((tm, tm), np.float32), -1), BF16)
    ustrict = jnp.asarray(np.triu(np.ones((LANES, LANES), np.float32), 1), BF16)
    return pl.pallas_call(
        _route_kernel,
        grid=(2, nt),
        in_specs=[pl.BlockSpec((tm, LANES), lambda p, i: (i, 0)), _full(lstrict.shape), _full(ustrict.shape)],
        out_specs=[pl.BlockSpec((tm, LANES), lambda p, i: (i * p, 0)), _full((nb, LANES)), _full((8, LANES))],
        out_shape=[jax.ShapeDtypeStruct((n, LANES), I32), jax.ShapeDtypeStruct((nb, LANES), I32),
                   jax.ShapeDtypeStruct((8, LANES), I32)],
        scratch_shapes=[pltpu.VMEM((1, LANES), F32), pltpu.VMEM((1, LANES), F32), pltpu.VMEM((1, LANES), F32)],
        compiler_params=_cparams(("arbitrary", "arbitrary")),
        name="route",
    )(idx, lstrict, ustrict)


def _dispatch_kernel(meta_ref, dest_ref, h_ref, xs_ref, zero_ref, sem, zsem, *, nch):
    i = pl.program_id(0)
    tm = h_ref.shape[0] // nch

    def tile(ref, r):
        return ref.at[pl.ds(pl.multiple_of(r * nch, nch), nch), :]

    def row_copy(t, kk):
        return pltpu.make_async_copy(tile(h_ref, t), tile(xs_ref, dest_ref[t * TOP_K + kk]), sem)

    def issue(t, carry):
        for kk in range(TOP_K):
            row_copy(t, kk).start(priority=kk % 2)
        return carry

    def drain(t, carry):
        for kk in range(TOP_K):
            row_copy(t, kk).wait()
        return carry

    lax.fori_loop(0, tm, issue, 0)

    @pl.when(i == pl.num_programs(0) - 1)
    def _():
        zero_ref[...] = jnp.zeros_like(zero_ref)
        blk = MOE_ROWS * nch

        def pad_copy(r):
            return pltpu.make_async_copy(tile(zero_ref, 0), tile(xs_ref, r), zsem)

        def blk_copy(b):
            return pltpu.make_async_copy(zero_ref, xs_ref.at[pl.ds(pl.multiple_of(b * blk, blk), blk), :], zsem)

        for e in range(N_EXPERTS):
            lo, hi = meta_ref[0, e], meta_ref[1, e]
            lax.fori_loop(lo, hi, lambda r, c: (pad_copy(r).start(), c)[1], 0)
            lax.fori_loop(lo, hi, lambda r, c: (pad_copy(r).wait(), c)[1], 0)
        n_used = meta_ref[2, N_EXPERTS - 1]
        n_blocks = xs_ref.shape[0] // blk
        lax.fori_loop(n_used, n_blocks, lambda b, c: (blk_copy(b).start(), c)[1], 0)
        lax.fori_loop(n_used, n_blocks, lambda b, c: (blk_copy(b).wait(), c)[1], 0)

    lax.fori_loop(0, tm, drain, 0)


def _dispatch(h2t, dest_flat, meta, n_rows, nch):
    n = h2t.shape[0] // nch
    tm = TM_DMA
    grid_spec = pltpu.PrefetchScalarGridSpec(
        num_scalar_prefetch=1,
        grid=(n // tm,),
        in_specs=[pl.BlockSpec((tm * TOP_K,), lambda i, meta: (i,), memory_space=pltpu.SMEM),
                  pl.BlockSpec((tm * nch, LANES), lambda i, meta: (i, 0))],
        out_specs=pl.BlockSpec(memory_space=pl.ANY),
        scratch_shapes=[pltpu.VMEM((MOE_ROWS * nch, LANES), F32), pltpu.SemaphoreType.DMA(()),
                        pltpu.SemaphoreType.DMA(())],
    )
    return pl.pallas_call(
        functools.partial(_dispatch_kernel, nch=nch),
        grid_spec=grid_spec,
        out_shape=jax.ShapeDtypeStruct((n_rows * nch, LANES), F32),
        compiler_params=_cparams(("arbitrary",)),
        name="dispatch",
    )(meta, dest_flat, h2t)


def _experts_kernel(be_ref, xs_ref, wu_ref, bu_ref, wd_ref, bd_ref, y_ref, wu_bf, wd_bf):
    b = pl.program_id(0)
    ff = wd_ref.shape[1]
    prev = be_ref[jnp.maximum(b - 1, 0)]

    @pl.when((b == 0) | (be_ref[b] != prev))
    def _():
        wu_bf[...] = wu_ref[0].astype(BF16)
        wd_bf[...] = wd_ref[0].astype(BF16)

    nch = wu_ref.shape[1] // LANES
    x = _load_token_tiles(xs_ref, xs_ref.shape[0] // nch, nch)
    gu = _dot(x.astype(BF16), wu_bf[...]) + bu_ref[0]
    g = jnp.minimum(gu[:, :ff], SWIGLU_LIMIT)
    u = jnp.clip(gu[:, ff:], -SWIGLU_LIMIT, SWIGLU_LIMIT)
    act = g * jax.nn.sigmoid(SWIGLU_ALPHA * g) * (u + 1.0)
    _store_token_tiles(y_ref, _dot(act.astype(BF16), wd_bf[...]) + bd_ref[0])


def _experts(xs, block_expert, w_up, b_up, w_down, b_down):
    e, d, ff2 = w_up.shape
    ff = w_down.shape[1]
    nch = d // LANES
    n_rows = xs.shape[0] // nch
    grid_spec = pltpu.PrefetchScalarGridSpec(
        num_scalar_prefetch=1,
        grid=(n_rows // MOE_ROWS,),
        in_specs=[pl.BlockSpec((MOE_ROWS * nch, LANES), lambda b, be: (b, 0)),
                  pl.BlockSpec((1, d, ff2), lambda b, be: (be[b], 0, 0)),
                  pl.BlockSpec((1, 1, ff2), lambda b, be: (be[b], 0, 0)),
                  pl.BlockSpec((1, ff, d), lambda b, be: (be[b], 0, 0)),
                  pl.BlockSpec((1, 1, d), lambda b, be: (be[b], 0, 0))],
        out_specs=pl.BlockSpec((MOE_ROWS * nch, LANES), lambda b, be: (b, 0)),
        scratch_shapes=[pltpu.VMEM((d, ff2), BF16), pltpu.VMEM((ff, d), BF16)],
    )
    return pl.pallas_call(
        _experts_kernel,
        grid_spec=grid_spec,
        out_shape=jax.ShapeDtypeStruct((n_rows * nch, LANES), F32),
        compiler_params=_cparams(("arbitrary",)),
        name="experts",
    )(block_expert, xs, w_up, b_up.reshape(e, 1, ff2), w_down, b_down.reshape(e, 1, d))


def _combine_kernel(dest_ref, dest_next_ref, x1_ref, gate_ref, p_ref, gple_ref, wpg_ref, wpp_ref, ys_ref, o_ref,
                    rows_ref, sem):
    i = pl.program_id(0)
    tm, d = x1_ref.shape
    nch = d // LANES
    slot = lax.rem(i, 2)

    def row_copy(idx_ref, buf, t, kk):
        src = pl.multiple_of(idx_ref[t * TOP_K + kk] * nch, nch)
        dst = pl.multiple_of(t * nch, nch)
        return pltpu.make_async_copy(ys_ref.at[pl.ds(src, nch), :], rows_ref.at[buf, kk, pl.ds(dst, nch), :],
                                     sem.at[buf])

    def gather(idx_ref, buf):
        def issue(t, carry):
            for kk in range(TOP_K):
                row_copy(idx_ref, buf, t, kk).start(priority=kk % 2)
            return carry
        lax.fori_loop(0, tm, issue, 0)

    @pl.when(i == 0)
    def _():
        gather(dest_ref, 0)

    @pl.when(i + 1 < pl.num_programs(0))
    def _():
        gather(dest_next_ref, 1 - slot)

    proj = _dot(p_ref[...].astype(BF16), wpp_ref[...])

    def drain(t, carry):
        for kk in range(TOP_K):
            row_copy(dest_ref, slot, t, kk).wait()
        return carry

    lax.fori_loop(0, tm, drain, 0)
    gates = gate_ref[...]
    moe = gates[:, 0:1] * _load_token_tiles(rows_ref.at[slot, 0], tm, nch)
    for kk in range(1, TOP_K):
        moe = moe + gates[:, kk:kk + 1] * _load_token_tiles(rows_ref.at[slot, kk], tm, nch)
    x2 = x1_ref[...] + moe
    hn = _rms_rows(x2, gple_ref[...]).astype(BF16)
    o_ref[...] = x2 + jax.nn.sigmoid(_dot(hn, wpg_ref[...])) * proj


def _combine(x1, gates, dest_flat, p2d, gple, wpg, wpp, ys):
    n, d = x1.shape
    tm = TM_DMA
    nt = n // tm
    row = lambda c: pl.BlockSpec((tm, c), lambda i: (i, 0))
    return pl.pallas_call(
        _combine_kernel,
        grid=(nt,),
        in_specs=[pl.BlockSpec((tm * TOP_K,), lambda i: (i,), memory_space=pltpu.SMEM),
                  pl.BlockSpec((tm * TOP_K,), lambda i: (jnp.minimum(i + 1, nt - 1),), memory_space=pltpu.SMEM),
                  row(d), row(LANES), row(p2d.shape[1]), _full(gple.shape), _full(wpg.shape), _full(wpp.shape),
                  pl.BlockSpec(memory_space=pl.ANY)],
        out_specs=row(d),
        out_shape=jax.ShapeDtypeStruct((n, d), F32),
        scratch_shapes=[pltpu.VMEM((2, TOP_K, tm * (d // LANES), LANES), F32), pltpu.SemaphoreType.DMA((2,))],
        compiler_params=_cparams(("arbitrary",)),
        name="combine",
    )(dest_flat, dest_flat, x1, gates, p2d, gple, wpg, wpp, ys)


def _rope_tables(seq):
    pos = jnp.arange(seq, dtype=F32)
    inv_freq = ROPE_THETA ** (-jnp.arange(0, MLA_ROPE, 2, dtype=F32) / MLA_ROPE)
    ang = pos[:, None] * inv_freq[None, :]
    c, s = jnp.cos(ang), jnp.sin(ang)
    reps = LANES // MLA_ROPE
    return jnp.tile(jnp.concatenate([c, c], -1), (1, reps)), jnp.tile(jnp.concatenate([-s, s], -1), (1, reps))


def _layer(x2d, p2d, batch, seq, cos, sin, attn_norm, w_in, q_a_norm, w_uq, kv_a_norm, w_ukv,
           mla_q_nope_norm, mla_q_rope_norm, mla_k_nope_norm, mla_k_rope_norm,
           fox_q_norm, fox_k_norm, fox_f_bias, w_o_mla, w_o_fox, w_o,
           ffn_norm, w_router, b_router, w_up, b_up, w_down, b_down, ple_norm, w_ple_gate, w_ple_proj):
    n, d = x2d.shape
    nf = FOX_HEADS * FOX_HD
    row1 = lambda v: v.reshape(1, -1).astype(F32)

    o = np.cumsum([0, MLA_Q_LORA, MLA_KV_LORA, MLA_ROPE, nf, nf, nf, FOX_HEADS, d, d])
    wa = jnp.concatenate([w_in[:, o[0]:o[3]], w_in[:, o[6]:o[7]],
                          jnp.zeros((d, ZA_COLS - int(o[3]) - FOX_HEADS), w_in.dtype)], axis=1).astype(BF16)
    wf = w_in[:, o[3]:o[6]].astype(BF16)
    wg = w_in[:, o[7]:o[9]].astype(BF16)
    za, zf, zg = _inproj(x2d, row1(attn_norm), wa, wf, wg)

    wq3 = w_uq.reshape(MLA_Q_LORA, MLA_HEADS, MLA_NOPE + MLA_ROPE)
    wuq = jnp.concatenate([wq3[:, :, :MLA_NOPE].reshape(MLA_Q_LORA, -1),
                           wq3[:, :, MLA_NOPE:].reshape(MLA_Q_LORA, -1)], axis=1).astype(BF16)
    wkv3 = w_ukv.reshape(MLA_KV_LORA, MLA_HEADS, MLA_NOPE + MLA_V)
    wukv = jnp.concatenate([wkv3[:, :, :MLA_NOPE].reshape(MLA_KV_LORA, -1),
                            wkv3[:, :, MLA_NOPE:].reshape(MLA_KV_LORA, -1)], axis=1).astype(BF16)
    mla_scale = (MLA_NOPE + MLA_ROPE) ** -0.5 * LOG2E
    fox_scale = FOX_HD ** -0.5 * LOG2E
    gkr = jnp.concatenate([mla_k_rope_norm.astype(F32), jnp.zeros((LANES - MLA_ROPE,), F32)])
    fb = jnp.zeros((LANES,), F32).at[FLOGIT_LANE:FLOGIT_LANE + FOX_HEADS].set(fox_f_bias.astype(F32))
    gains = [row1(q_a_norm), row1(kv_a_norm),
             row1(jnp.tile(mla_q_nope_norm, MLA_HEADS) * mla_scale),
             row1(jnp.tile(mla_q_rope_norm, MLA_HEADS) * mla_scale),
             row1(jnp.tile(mla_k_nope_norm, MLA_HEADS)), row1(gkr),
             row1(jnp.tile(fox_q_norm, FOX_HEADS) * fox_scale), row1(jnp.tile(fox_k_norm, FOX_HEADS)), row1(fb)]
    qm, km, vm, qf, kf, vf = _prep(za, zf, cos, sin, wuq, wukv, gains, batch, seq)

    o_mla = _attention(qm, km, vm, hpb=4, dv=MLA_V).reshape(n, MLA_HEADS * MLA_V)
    o_fox = _attention(qf, kf, vf, hpb=8, dv=FOX_HD).reshape(n, nf)

    wr = jnp.concatenate([w_router.astype(F32), jnp.zeros((d, LANES - N_EXPERTS), F32)], axis=1)
    wr_hi = wr.astype(BF16)
    wr = jnp.concatenate([wr_hi, (wr - wr_hi.astype(F32)).astype(BF16)], axis=1)
    br = jnp.concatenate([b_router.astype(F32), jnp.zeros((LANES - N_EXPERTS,), F32)]).reshape(1, LANES)
    x1, h2, idx, gates = _outproj(x2d, o_mla, o_fox, zg, w_o_mla.astype(BF16), w_o_fox.astype(BF16),
                                  w_o.astype(BF16), row1(ffn_norm), wr, br)

    n_blocks = n * TOP_K // MOE_ROWS + N_EXPERTS
    dest, be, meta = _route(idx, n_blocks)
    dest_flat = dest[:, :TOP_K].reshape(-1)
    xs = _dispatch(h2, dest_flat, meta, n_blocks * MOE_ROWS, d // LANES)
    ys = _experts(xs, be[:n_blocks, 0], w_up, b_up, w_down, b_down)
    return _combine(x1, gates, dest_flat, p2d, row1(ple_norm), w_ple_gate.astype(BF16),
                    w_ple_proj.astype(BF16), ys)


def kernel(x, p, attn_norm, w_in, q_a_norm, w_uq, kv_a_norm, w_ukv, mla_q_nope_norm, mla_q_rope_norm,
           mla_k_nope_norm, mla_k_rope_norm, fox_q_norm, fox_k_norm, fox_f_bias, w_o_mla, w_o_fox, w_o,
           ffn_norm, w_router, b_router, w_up, b_up, w_down, b_down, ple_norm, w_ple_gate, w_ple_proj):
    batch, seq, d = x.shape
    depth = p.shape[0]
    cos, sin = _rope_tables(seq)
    x2d = x.reshape(batch * seq, d)
    for i in range(depth):
        x2d = _layer(x2d, p[i].reshape(batch * seq, -1), batch, seq, cos, sin,
                     attn_norm[i], w_in[i], q_a_norm[i], w_uq[i], kv_a_norm[i], w_ukv[i],
                     mla_q_nope_norm[i], mla_q_rope_norm[i], mla_k_nope_norm[i], mla_k_rope_norm[i],
                     fox_q_norm[i], fox_k_norm[i], fox_f_bias[i], w_o_mla[i], w_o_fox[i], w_o[i],
                     ffn_norm[i], w_router[i], b_router[i], w_up[i], b_up[i], w_down[i], b_down[i],
                     ple_norm[i], w_ple_gate[i], w_ple_proj[i])
    return x2d.reshape(batch, seq, d)
```
